```python
import jax, jax.numpy as jnp
from jax import lax
import numpy as np

D_MODEL = 1024
BATCH = 32
SEQ = 2048
DEPTH = 1

HEAD_DIM = 64
NSA_HEADS = 8
NSA_KV_HEADS = 2
NSA_GROUP = NSA_HEADS // NSA_KV_HEADS
NSA_CMP_BLOCK = 32
NSA_CMP_STRIDE = 16
NSA_SEL_BLOCK = 64
NSA_SEL_TOPN = 16
NSA_WINDOW = 512
NSA_Q_CHUNK = 32
NSA_FORCE_BONUS = 1e4
MOBA_HEADS = 8
MOBA_BLOCK = 256
MOBA_TOPK = 3
MOBA_Q_CHUNK = 16
D_FF = 2816
CONV_WIDTH = 3
ROPE_THETA = 10000.0
NORM_EPS = 1e-6
NEG_INF = -1e30

NSA_WIDTH = NSA_HEADS * HEAD_DIM
NSA_KV_WIDTH = NSA_KV_HEADS * HEAD_DIM
MOBA_WIDTH = MOBA_HEADS * HEAD_DIM
IN_SIZES = (NSA_WIDTH, NSA_KV_WIDTH, NSA_KV_WIDTH, NSA_KV_WIDTH, NSA_KV_WIDTH, NSA_KV_WIDTH, NSA_KV_WIDTH,
            3 * NSA_HEADS, MOBA_WIDTH, MOBA_WIDTH, MOBA_WIDTH, D_MODEL, D_MODEL)
IN_WIDTH = sum(IN_SIZES)

kernel_name = "hybrid_nsa_moba_convffn_adaln"


def rms_norm(x, g):
    xf = x.astype(jnp.float32)
    y = xf * lax.rsqrt(jnp.mean(xf * xf, axis=-1, keepdims=True) + NORM_EPS)
    return (y * g.astype(jnp.float32)).astype(x.dtype)


def rope(x, positions):
    half = x.shape[-1] // 2
    inv_freq = ROPE_THETA ** (-jnp.arange(half, dtype=jnp.float32) / half)
    ang = positions.astype(jnp.float32)[:, None, :, None] * inv_freq
    cos, sin = jnp.cos(ang), jnp.sin(ang)
    xf = x.astype(jnp.float32)
    x1, x2 = xf[..., :half], xf[..., half:]
    return jnp.concatenate([x1 * cos - x2 * sin, x2 * cos + x1 * sin], axis=-1).astype(x.dtype)


def masked_softmax(s, mask):
    s = jnp.where(mask, s, NEG_INF)
    m = jnp.max(s, axis=-1, keepdims=True)
    p = jnp.where(mask, jnp.exp(s - m), 0.0)
    return p / jnp.maximum(jnp.sum(p, axis=-1, keepdims=True), 1e-30)


def split_heads(t, n):
    b, s, _ = t.shape
    return t.reshape(b, s, n, HEAD_DIM).transpose(0, 2, 1, 3)


def merge_heads(t):
    b, n, s, d = t.shape
    return t.transpose(0, 2, 1, 3).reshape(b, s, n * d)


def nsa_compress(x, w_pos, w1, w2):
    s = x.shape[2]
    n_cmp = (s - NSA_CMP_BLOCK) // NSA_CMP_STRIDE + 1
    idx = np.arange(n_cmp)[:, None] * NSA_CMP_STRIDE + np.arange(NSA_CMP_BLOCK)[None, :]
    blocks = x[:, :, idx] + w_pos
    flat = blocks.reshape(blocks.shape[0], blocks.shape[1], n_cmp, NSA_CMP_BLOCK * HEAD_DIM)
    return jax.nn.gelu(flat @ w1) @ w2


def nsa_attention(q, k_cmp, v_cmp, k_slc, v_slc, k_win, v_win, gates):
    b, h, s, dh = q.shape
    g_kv, r = NSA_KV_HEADS, NSA_GROUP
    scale = HEAD_DIM ** -0.5
    qg = q.reshape(b, g_kv, r, s, dh)
    t_pos = jnp.arange(s)

    n_cmp = k_cmp.shape[2]
    cmp_end = jnp.arange(n_cmp) * NSA_CMP_STRIDE + NSA_CMP_BLOCK - 1
    s_cmp = jnp.einsum('bgrsd,bgcd->bgrsc', qg, k_cmp).astype(jnp.float32) * scale
    p_cmp = masked_softmax(s_cmp, cmp_end[None, :] <= t_pos[:, None])
    o_cmp = jnp.einsum('bgrsc,bgcd->bgrsd', p_cmp.astype(v_cmp.dtype), v_cmp).reshape(b, h, s, dh)

    n_sb = s // NSA_SEL_BLOCK
    c_start = np.arange(n_cmp)[:, None] * NSA_CMP_STRIDE
    js = np.arange(n_sb)[None, :]
    overlap = ((c_start < (js + 1) * NSA_SEL_BLOCK) & (c_start + NSA_CMP_BLOCK > js * NSA_SEL_BLOCK)).astype(np.float32)
    imp = jnp.einsum('bgrsc,cj->bgsj', p_cmp, jnp.asarray(overlap))
    own = t_pos // NSA_SEL_BLOCK
    jb = jnp.arange(n_sb)[None, :]
    forced = (jb == 0) | (jb == own[:, None]) | (jb == own[:, None] - 1)
    imp = jnp.where(jb <= own[:, None], imp + jnp.where(forced, NSA_FORCE_BONUS, 0.0), NEG_INF)
    n_sel = min(NSA_SEL_TOPN, n_sb)
    _, sel_idx = lax.top_k(imp, n_sel)

    k_blocks = k_slc.reshape(b, g_kv, n_sb, NSA_SEL_BLOCK, dh)
    v_blocks = v_slc.reshape(b, g_kv, n_sb, NSA_SEL_BLOCK, dh)
    pad = ((0, 0), (0, 0), (NSA_WINDOW, 0), (0, 0))
    kw_pad = jnp.pad(k_win, pad)
    vw_pad = jnp.pad(v_win, pad)

    qc = NSA_Q_CHUNK
    nc = s // qc
    band = NSA_WINDOW + qc
    q_chunks = qg.reshape(b, g_kv, r, nc, qc, dh).transpose(3, 0, 1, 2, 4, 5)
    idx_chunks = sel_idx.reshape(b, g_kv, nc, qc, n_sel).transpose(2, 0, 1, 3, 4)
    starts = jnp.arange(nc) * qc
    bi = jnp.arange(b)[:, None, None, None]
    gi = jnp.arange(g_kv)[None, :, None, None]

    def chunk(args):
        q_c, idx_c, start = args
        t = start + jnp.arange(qc)
        kb = k_blocks[bi, gi, idx_c]
        vb = v_blocks[bi, gi, idx_c]
        s_sel = jnp.einsum('bgrqd,bgqnkd->bgrqnk', q_c, kb).astype(jnp.float32) * scale
        kp = idx_c[..., None] * NSA_SEL_BLOCK + jnp.arange(NSA_SEL_BLOCK)
        m_sel = (kp <= t[:, None, None])[:, :, None].reshape(b, g_kv, 1, qc, n_sel * NSA_SEL_BLOCK)
        p_sel = masked_softmax(s_sel.reshape(b, g_kv, r, qc, n_sel * NSA_SEL_BLOCK), m_sel)
        o_sel = jnp.einsum('bgrqnk,bgqnkd->bgrqd', p_sel.reshape(s_sel.shape).astype(vb.dtype), vb)
        kwb = lax.dynamic_slice_in_dim(kw_pad, start, band, axis=2)
        vwb = lax.dynamic_slice_in_dim(vw_pad, start, band, axis=2)
        kpw = start - NSA_WINDOW + jnp.arange(band)
        diff = t[:, None] - kpw[None, :]
        m_win = (kpw[None, :] >= 0) & (diff >= 0) & (diff < NSA_WINDOW)
        s_win = jnp.einsum('bgrqd,bgkd->bgrqk', q_c, kwb).astype(jnp.float32) * scale
        o_win = jnp.einsum('bgrqk,bgkd->bgrqd', masked_softmax(s_win, m_win).astype(vwb.dtype), vwb)
        return o_sel, o_win

    o_sel, o_win = lax.map(chunk, (q_chunks, idx_chunks, starts))
    unchunk = lambda o: o.transpose(1, 2, 3, 0, 4, 5).reshape(b, h, s, dh)
    return gates[..., 0:1] * o_cmp + gates[..., 1:2] * unchunk(o_sel) + gates[..., 2:3] * unchunk(o_win)


def moba_attention(q, k, v):
    b, h, s, dh = q.shape
    blk = MOBA_BLOCK
    scale = HEAD_DIM ** -0.5
    nb = -(-s // blk)
    pad = ((0, 0), (0, 0), (0, nb * blk - s), (0, 0))
    k_pad = jnp.pad(k, pad)
    v_pad = jnp.pad(v, pad)
    k_blocks = k_pad.reshape(b, h, nb, blk, dh)
    v_blocks = v_pad.reshape(b, h, nb, blk, dh)
    own = jnp.arange(s) // blk
    n_top = min(MOBA_TOPK, nb - 1)
    qc = MOBA_Q_CHUNK
    nc = s // qc
    q_chunks = q.reshape(b, h, nc, qc, dh).transpose(2, 0, 1, 3, 4)
    starts = jnp.arange(nc) * qc
    if n_top > 0:
        k_mean = jnp.mean(k_blocks.astype(jnp.float32), axis=3)
        gate = jnp.einsum('bhsd,bhjd->bhsj', q.astype(jnp.float32), k_mean)
        past = jnp.arange(nb)[None, :] < own[:, None]
        _, sel_idx = lax.top_k(jnp.where(past, gate, NEG_INF), n_top)
        idx_chunks = sel_idx.reshape(b, h, nc, qc, n_top).transpose(2, 0, 1, 3, 4)
    else:
        idx_chunks = jnp.zeros((nc, b, h, qc, 0), jnp.int32)
    bi = jnp.arange(b)[:, None, None, None]
    hi = jnp.arange(h)[None, :, None, None]

    def chunk(args):
        q_c, idx_c, start = args
        t = start + jnp.arange(qc)
        cur = start // blk
        blk_start = cur * blk
        k_own = lax.dynamic_slice_in_dim(k_pad, blk_start, blk, axis=2)
        v_own = lax.dynamic_slice_in_dim(v_pad, blk_start, blk, axis=2)
        m_own = jnp.broadcast_to((blk_start + jnp.arange(blk))[None, :] <= t[:, None], (b, h, qc, blk))
        s_own = jnp.einsum('bhqd,bhkd->bhqk', q_c, k_own).astype(jnp.float32) * scale
        if n_top == 0:
            p = masked_softmax(s_own, m_own).astype(v_own.dtype)
            return jnp.einsum('bhqk,bhkd->bhqd', p, v_own)
        kg = k_blocks[bi, hi, idx_c]
        vg = v_blocks[bi, hi, idx_c]
        s_past = jnp.einsum('bhqd,bhqnkd->bhqnk', q_c, kg).astype(jnp.float32).reshape(b, h, qc, n_top * blk) * scale
        m_past = jnp.broadcast_to((idx_c < cur)[..., None], (b, h, qc, n_top, blk)).reshape(b, h, qc, n_top * blk)
        p = masked_softmax(jnp.concatenate([s_past, s_own], axis=-1),
                           jnp.concatenate([m_past, m_own], axis=-1)).astype(v.dtype)
        p_past = p[..., :n_top * blk].reshape(b, h, qc, n_top, blk)
        return (jnp.einsum('bhqnk,bhqnkd->bhqd', p_past, vg)
                + jnp.einsum('bhqk,bhkd->bhqd', p[..., n_top * blk:], v_own))

    o = lax.map(chunk, (q_chunks, idx_chunks, starts))
    return o.transpose(1, 2, 0, 3, 4).reshape(b, h, s, dh)


def causal_dwconv(a, w, bias):
    y = lax.conv_general_dilated(a, w[:, None, :].astype(a.dtype), window_strides=(1,),
                                 padding=[(CONV_WIDTH - 1, 0)],
                                 dimension_numbers=('NWC', 'WIO', 'NWC'),
                                 feature_group_count=a.shape[-1])
    return y + bias


def setup_inputs(seed: int = 0) -> dict:
    key = jax.random.key(seed)
    ks = jax.random.split(key, 32)
    f32 = jnp.float32

    def nrm(k, shape, scale):
        return jax.random.normal(k, shape, f32) * scale

    def gain(k, n):
        return 1.0 + 0.05 * jax.random.normal(k, (DEPTH, n), f32)

    L = NSA_CMP_BLOCK
    offsets = jax.random.randint(ks[2], (BATCH, 1), 0, 1024)
    positions = (jnp.arange(SEQ)[None, :] + offsets).astype(jnp.int32)
    return {
        "x": nrm(ks[0], (BATCH, SEQ, D_MODEL), 1.0),
        "c": nrm(ks[1], (BATCH, D_MODEL), 1.0),
        "positions": positions,
        "w_ada": nrm(ks[3], (DEPTH, D_MODEL, 6 * D_MODEL), 0.5 * D_MODEL ** -0.5),
        "b_ada": nrm(ks[4], (DEPTH, 6 * D_MODEL), 0.01),
        "g_attn_norm": gain(ks[5], D_MODEL),
        "w_in": nrm(ks[6], (DEPTH, D_MODEL, IN_WIDTH), D_MODEL ** -0.5),
        "g_q_nsa": gain(ks[7], HEAD_DIM),
        "g_k_cmp": gain(ks[8], HEAD_DIM),
        "g_k_slc": gain(ks[9], HEAD_DIM),
        "g_k_win": gain(ks[10], HEAD_DIM),
        "cmp_k_pos": nrm(ks[11], (DEPTH, L, HEAD_DIM), 0.02),
        "cmp_k_w1": nrm(ks[12], (DEPTH, L * HEAD_DIM, HEAD_DIM), (L * HEAD_DIM) ** -0.5),
        "cmp_k_w2": nrm(ks[13], (DEPTH, HEAD_DIM, HEAD_DIM), HEAD_DIM ** -0.5),
        "cmp_v_pos": nrm(ks[14], (DEPTH, L, HEAD_DIM), 0.02),
        "cmp_v_w1": nrm(ks[15], (DEPTH, L * HEAD_DIM, HEAD_DIM), (L * HEAD_DIM) ** -0.5),
        "cmp_v_w2": nrm(ks[16], (DEPTH, HEAD_DIM, HEAD_DIM), HEAD_DIM ** -0.5),
        "g_q_moba": gain(ks[17], HEAD_DIM),
        "g_k_moba": gain(ks[18], HEAD_DIM),
        "w_branch_nsa": nrm(ks[19], (DEPTH, NSA_WIDTH, D_MODEL), NSA_WIDTH ** -0.5),
        "w_branch_moba": nrm(ks[20], (DEPTH, MOBA_WIDTH, D_MODEL), MOBA_WIDTH ** -0.5),
        "w_out": nrm(ks[21], (DEPTH, D_MODEL, D_MODEL), D_MODEL ** -0.5),
        "g_ffn_norm": gain(ks[22], D_MODEL),
        "w_ffn_up": nrm(ks[23], (DEPTH, D_MODEL, 2 * D_FF), D_MODEL ** -0.5),
        "conv_w": nrm(ks[24], (DEPTH, CONV_WIDTH, D_FF), CONV_WIDTH ** -0.5),
        "conv_b": nrm(ks[25], (DEPTH, D_FF), 0.01),
        "w_ffn_down": nrm(ks[26], (DEPTH, D_FF, D_MODEL), D_FF ** -0.5),
    }


def reference(x, c, positions, w_ada, b_ada, g_attn_norm, w_in, g_q_nsa, g_k_cmp, g_k_slc, g_k_win,
              cmp_k_pos, cmp_k_w1, cmp_k_w2, cmp_v_pos, cmp_v_w1, cmp_v_w2, g_q_moba, g_k_moba,
              w_branch_nsa, w_branch_moba, w_out, g_ffn_norm, w_ffn_up, conv_w, conv_b, w_ffn_down):
    b, s, _ = x.shape
    offsets = np.cumsum(IN_SIZES)[:-1].tolist()
    for l in range(DEPTH):
        mod = (c @ w_ada[l] + b_ada[l])[:, None, :]
        sh1, sc1, gt1, sh2, sc2, gt2 = jnp.split(mod, 6, axis=-1)

        h = rms_norm(x, g_attn_norm[l]) * (1 + sc1) + sh1
        (q_a, kc, vc, ksl, vsl, kwn, vwn, gate_nsa, q_b, k_b, v_b, gate_a, gate_b) = jnp.split(h @ w_in[l], offsets, axis=-1)

        q_a = rope(rms_norm(split_heads(q_a, NSA_HEADS), g_q_nsa[l]), positions)
        kc = rms_norm(nsa_compress(rope(split_heads(kc, NSA_KV_HEADS), positions),
                                   cmp_k_pos[l], cmp_k_w1[l], cmp_k_w2[l]), g_k_cmp[l])
        vc = nsa_compress(split_heads(vc, NSA_KV_HEADS), cmp_v_pos[l], cmp_v_w1[l], cmp_v_w2[l])
        ksl = rope(rms_norm(split_heads(ksl, NSA_KV_HEADS), g_k_slc[l]), positions)
        kwn = rope(rms_norm(split_heads(kwn, NSA_KV_HEADS), g_k_win[l]), positions)
        g_nsa = jax.nn.sigmoid(gate_nsa).reshape(b, s, NSA_HEADS, 3).transpose(0, 2, 1, 3)
        o_a = merge_heads(nsa_attention(q_a, kc, vc, ksl, split_heads(vsl, NSA_KV_HEADS),
                                        kwn, split_heads(vwn, NSA_KV_HEADS), g_nsa))

        q_b = rope(rms_norm(split_heads(q_b, MOBA_HEADS), g_q_moba[l]), positions)
        k_b = rope(rms_norm(split_heads(k_b, MOBA_HEADS), g_k_moba[l]), positions)
        o_b = merge_heads(moba_attention(q_b, k_b, split_heads(v_b, MOBA_HEADS)))

        mixed = (jax.nn.sigmoid(gate_a) * (o_a @ w_branch_nsa[l])
                 + jax.nn.sigmoid(gate_b) * (o_b @ w_branch_moba[l]))
        x = x + gt1 * (mixed @ w_out[l])

        h = rms_norm(x, g_ffn_norm[l]) * (1 + sc2) + sh2
        a, v = jnp.split(h @ w_ffn_up[l], 2, axis=-1)
        y = jax.nn.gelu(causal_dwconv(a, conv_w[l], conv_b[l])) * v
        x = x + gt2 * (y @ w_ffn_down[l])
    return x
```

```python
import functools

import numpy as np
import jax
import jax.numpy as jnp
from jax import lax
from jax.experimental import pallas as pl
from jax.experimental.pallas import tpu as pltpu

F32 = jnp.float32
BF16 = jnp.bfloat16

D_MODEL = 1024
HEAD_DIM = 64
NSA_HEADS = 8
NSA_KV_HEADS = 2
NSA_CMP_BLOCK = 32
NSA_CMP_STRIDE = 16
NSA_SEL_BLOCK = 64
NSA_SEL_TOPN = 16
NSA_WINDOW = 512
NSA_FORCE_BONUS = 1e4
MOBA_HEADS = 8
MOBA_BLOCK = 256
MOBA_TOPK = 3
D_FF = 2816
CONV_WIDTH = 3
ROPE_THETA = 10000.0
NORM_EPS = 1e-6
NEG_INF = -1e30

LANES = 128
SCALE = HEAD_DIM ** -0.5
MASK_BIAS = -(2.0 ** 100)
VMEM_LIMIT = 56 * 1024 * 1024

TM_PROJ = 512
TQ = 256
FF_CHUNK = 256
N_FF_CHUNKS = D_FF // FF_CHUNK

NSA_WIDTH = NSA_HEADS * HEAD_DIM
MOBA_WIDTH = MOBA_HEADS * HEAD_DIM
KV_WIDTH = NSA_KV_HEADS * HEAD_DIM
IN_SIZES = (NSA_WIDTH, KV_WIDTH, KV_WIDTH, KV_WIDTH, KV_WIDTH, KV_WIDTH, KV_WIDTH,
            3 * NSA_HEADS, MOBA_WIDTH, MOBA_WIDTH, MOBA_WIDTH, D_MODEL, D_MODEL)
IN_OFFSETS = np.concatenate([[0], np.cumsum(IN_SIZES)]).tolist()

NSA_HEAD_ORDER = (0, 4, 1, 5, 2, 6, 3, 7)


def _dot(a, b):
    return jnp.dot(a, b, preferred_element_type=F32)


def _dot_nt(a, b):
    return lax.dot_general(a, b, (((1,), (1,)), ((), ())), preferred_element_type=F32)


def _split_bf16(v):
    hi = v.astype(BF16)
    lo = (v - hi.astype(F32)).astype(BF16)
    return hi, lo


def _cparams(*sem):
    return pltpu.CompilerParams(dimension_semantics=sem, vmem_limit_bytes=VMEM_LIMIT)


def _const_spec(shape):
    n = len(shape)
    return pl.BlockSpec(shape, lambda *_: (0,) * n, pipeline_mode=pl.Buffered(1))


def _adaln_norm(x, g, sc, sh):
    y = x * lax.rsqrt(jnp.mean(x * x, axis=-1, keepdims=True) + NORM_EPS)
    return (y * g) * (1.0 + sc) + sh


def _mod_kernel(c_ref, w_ref, b_ref, o_ref):
    o_ref[...] = jnp.dot(c_ref[...], w_ref[...], preferred_element_type=F32,
                         precision=lax.Precision.HIGHEST) + b_ref[...]


def _mod_call(c, w_ada, b_ada):
    b, d = c.shape
    n = w_ada.shape[1]
    tn = D_MODEL
    return pl.pallas_call(
        _mod_kernel,
        grid=(n // tn,),
        in_specs=[pl.BlockSpec((b, d), lambda j: (0, 0)),
                  pl.BlockSpec((d, tn), lambda j: (0, j)),
                  pl.BlockSpec((1, tn), lambda j: (0, j))],
        out_specs=pl.BlockSpec((b, tn), lambda j: (0, j)),
        out_shape=jax.ShapeDtypeStruct((b, n), F32),
        compiler_params=_cparams("parallel"),
        name="mod",
    )(c, w_ada, b_ada.reshape(1, n))


_P_QN, _P_QM, _P_KM, _P_VM = 0, 512, 1024, 1536
_P_KC, _P_VC, _P_KSL, _P_VSL, _P_KWN, _P_VWN, _P_GN = 2048, 2176, 2304, 2432, 2560, 2688, 2816
_P_WIDTH = 2944


def _inproj_kernel(x_ref, sc_ref, sh_ref, g_ref, pos_ref, invf_ref, w_ref, gains_ref, bd_ref,
                   qn_ref, qm_ref, km_ref, vm_ref, kc_ref, vc_ref, ksl_ref, vsl_ref, kwn_ref,
                   vwn_ref, gn_ref):
    x = x_ref[0]
    tm = x.shape[0]
    h = _adaln_norm(x, g_ref[...], sc_ref[...], sh_ref[...]).astype(BF16)

    ang = pos_ref[0].astype(F32) * invf_ref[...]
    cos = jnp.cos(ang)
    sin = jnp.sin(ang)
    lane = lax.broadcasted_iota(jnp.int32, (tm, LANES), 1)
    first = (lane & (HEAD_DIM // 2)) == 0
    sin_signed = jnp.where(first, -sin, sin)
    bd = bd_ref[...]

    def head_norm(y, gain):
        ms = _dot((y * y).astype(BF16), bd)
        return y * lax.rsqrt(ms + NORM_EPS) * gain

    def rope(y):
        partner = jnp.where(first, pltpu.roll(y, LANES - HEAD_DIM // 2, 1),
                            pltpu.roll(y, HEAD_DIM // 2, 1))
        return y * cos + partner * sin_signed

    def proj(off, width):
        return _dot(h, w_ref[:, off:off + width])

    def wide(off, out_ref, gain_row, scale):
        acc = proj(off, 4 * LANES)
        gain = gains_ref[gain_row:gain_row + 1, :]
        for p in range(4):
            y = rope(head_norm(acc[:, p * LANES:(p + 1) * LANES], gain))
            if scale != 1.0:
                y = y * scale
            out_ref[0, :, p * LANES:(p + 1) * LANES] = y.astype(BF16)

    wide(_P_QN, qn_ref, 0, SCALE)
    wide(_P_QM, qm_ref, 1, SCALE)
    wide(_P_KM, km_ref, 2, 1.0)
    vm_ref[0] = proj(_P_VM, 4 * LANES).astype(BF16)

    small = proj(_P_KC, 7 * LANES)
    kc_ref[0] = rope(small[:, 0:LANES]).astype(BF16)
    vc_ref[0] = small[:, LANES:2 * LANES].astype(BF16)
    ksl_ref[0] = rope(head_norm(small[:, 2 * LANES:3 * LANES], gains_ref[3:4, :])).astype(BF16)
    vsl_ref[0] = small[:, 3 * LANES:4 * LANES].astype(BF16)
    kwn_ref[0] = rope(head_norm(small[:, 4 * LANES:5 * LANES], gains_ref[4:5, :])).astype(BF16)
    vwn_ref[0] = small[:, 5 * LANES:6 * LANES].astype(BF16)
    gn_ref[0] = jax.nn.sigmoid(small[:, 6 * LANES:7 * LANES])


def _inproj_call(x, mod4, g_attn, pos3, inv_freq, w_p, gains, bd):
    b, s, d = x.shape
    tm = min(TM_PROJ, s)
    tok = lambda width: pl.BlockSpec((1, tm, width), lambda bi, i: (bi, i, 0))
    modspec = lambda k: pl.BlockSpec((None, None, 1, d), lambda bi, i: (bi, k, 0, 0))
    shapes = [jax.ShapeDtypeStruct((b, s, 4 * LANES), BF16)] * 4 \
        + [jax.ShapeDtypeStruct((b, s, LANES), BF16)] * 6 \
        + [jax.ShapeDtypeStruct((b, s, LANES), F32)]
    return pl.pallas_call(
        _inproj_kernel,
        grid=(b, s // tm),
        in_specs=[tok(d), modspec(1), modspec(0), _const_spec((1, d)), tok(1),
                  _const_spec((1, LANES)), _const_spec((d, _P_WIDTH)),
                  _const_spec((8, LANES)), _const_spec((LANES, LANES))],
        out_specs=[tok(4 * LANES)] * 4 + [tok(LANES)] * 7,
        out_shape=shapes,
        compiler_params=_cparams("parallel", "parallel"),
        name="inproj",
    )(x, mod4, mod4, g_attn, pos3, inv_freq, w_p, gains, bd)


def _compress_kernel(xk_ref, xv_ref, posk_ref, posv_ref, w1k_ref, w1v_ref, w2k_ref, w2v_ref,
                     gain_ref, bd_ref, ko_ref, vo_ref):
    def mlp(x_ref, pos_ref, w1_ref, w2_ref):
        acc = None
        for g in range(NSA_KV_HEADS):
            xg = x_ref[0, g].astype(F32)
            xa = (xg + pos_ref[0:1, :]).astype(BF16)
            xb = (xg + pos_ref[1:2, :]).astype(BF16)
            a = _dot(xa, w1_ref[0, g])
            bm = _dot(xb, w1_ref[1, g])
            part = a + pltpu.roll(bm, bm.shape[0] - 1, 0)
            acc = part if acc is None else acc + part
        return _dot(jax.nn.gelu(acc).astype(BF16), w2_ref[...])

    k = mlp(xk_ref, posk_ref, w1k_ref, w2k_ref)
    ms = _dot((k * k).astype(BF16), bd_ref[...])
    ko_ref[0] = (k * lax.rsqrt(ms + NORM_EPS) * gain_ref[...]).astype(BF16)
    vo_ref[0] = mlp(xv_ref, posv_ref, w1v_ref, w2v_ref).astype(BF16)


def _compress_call(xk, xv, posk, posv, w1k, w1v, w2k, w2v, gain, bd):
    b, g, n, w = xk.shape
    xspec = pl.BlockSpec((1, g, n, w), lambda bi: (bi, 0, 0, 0))
    ospec = pl.BlockSpec((1, n, LANES), lambda bi: (bi, 0, 0))
    return pl.pallas_call(
        _compress_kernel,
        grid=(b,),
        in_specs=[xspec, xspec, _const_spec(posk.shape), _const_spec(posv.shape),
                  _const_spec(w1k.shape), _const_spec(w1v.shape), _const_spec(w2k.shape),
                  _const_spec(w2v.shape), _const_spec((1, LANES)), _const_spec((LANES, LANES))],
        out_specs=[ospec, ospec],
        out_shape=[jax.ShapeDtypeStruct((b, n, LANES), BF16)] * 2,
        compiler_params=_cparams("parallel"),
        name="compress",
    )(xk, xv, posk, posv, w1k, w1v, w2k, w2v, gain, bd)


def _half_masks(tq):
    lane = lax.broadcasted_iota(jnp.int32, (tq, LANES), 1)
    return lane < HEAD_DIM


def _take_half(q2, half0, g):
    qf = q2.astype(F32)
    keep = half0 if g == 0 else jnp.logical_not(half0)
    return jnp.where(keep, qf, 0.0).astype(BF16)


def _eye_bf16(n):
    r = lax.broadcasted_iota(jnp.int32, (n, n), 0)
    c = lax.broadcasted_iota(jnp.int32, (n, n), 1)
    return jnp.where(r == c, 1.0, 0.0).astype(BF16)


def _rank_counts(vals, jb, n):
    counts = jnp.zeros_like(vals)
    for j in range(n):
        row = vals[j:j + 1, :]
        beats = jnp.where(vals > row, 1.0,
                          jnp.where(vals == row, jnp.where(jb < j, 1.0, 0.0), 0.0))
        cnt = jnp.sum(beats, axis=0, keepdims=True)
        counts = jnp.where(jb == j, cnt, counts)
    return counts


def _bias_to_rows(bias_t, eye):
    n, tq = bias_t.shape
    padded = jnp.concatenate([bias_t, jnp.zeros((LANES - n, tq), F32)], axis=0).astype(BF16)
    return _dot_nt(eye, padded).astype(BF16)


def _flash_first(q_aug, k_aug, v, mask):
    s = _dot_nt(q_aug, k_aug)
    s = jnp.where(mask, s, NEG_INF)
    m = jnp.max(s, axis=-1, keepdims=True)
    p = jnp.exp(s - m)
    l = jnp.sum(p, axis=-1, keepdims=True)
    acc = _dot(p.astype(BF16), v)
    return m, l, acc


def _flash_next(q_aug, k_aug, v, mask, carry):
    m, l, acc = carry
    s = _dot_nt(q_aug, k_aug)
    if mask is not None:
        s = jnp.where(mask, s, NEG_INF)
    m_new = jnp.maximum(m, jnp.max(s, axis=-1, keepdims=True))
    alpha = jnp.exp(m - m_new)
    p = jnp.exp(s - m_new)
    l = alpha * l + jnp.sum(p, axis=-1, keepdims=True)
    acc = alpha * acc + _dot(p.astype(BF16), v)
    return m_new, l, acc


def _chunk(ref, c, tk):
    return ref[0, pl.ds(pl.multiple_of(c * tk, tk), tk), :]


def _onehot_chunk(ref, c, tk):
    return ref[pl.ds(pl.multiple_of(c * tk, tk), tk), :]


def _nsa_cmp_kernel(q_ref, gate_ref, kc_ref, vc_ref, ovt_ref, eg_ref, ocmp_ref, selb_ref):
    tq = q_ref.shape[1]
    t0 = pl.program_id(1) * tq
    g_hi, g_lo = _split_bf16(gate_ref[0])
    gexp = _dot(g_hi, eg_ref[...]) + _dot(g_lo, eg_ref[...])

    lane = lax.broadcasted_iota(jnp.int32, (tq, LANES), 1)
    row_t = t0 + lax.broadcasted_iota(jnp.int32, (tq, LANES), 0)
    visible = lane * NSA_CMP_STRIDE + (NSA_CMP_BLOCK - 1) <= row_t
    half0 = lane < HEAD_DIM
    kc = kc_ref[0]
    vc = vc_ref[0]

    psum = [None, None]
    for r in range(NSA_HEADS // NSA_KV_HEADS):
        q2 = q_ref[0, :, r * LANES:(r + 1) * LANES]
        o_pair = None
        for g in range(NSA_KV_HEADS):
            s = _dot_nt(_take_half(q2, half0, g), kc)
            s = jnp.where(visible, s, NEG_INF)
            m = jnp.max(s, axis=-1, keepdims=True)
            p = jnp.where(visible, jnp.exp(s - m), 0.0)
            p = p / jnp.maximum(jnp.sum(p, axis=-1, keepdims=True), 1e-30)
            psum[g] = p if psum[g] is None else psum[g] + p
            o = _dot(p.astype(BF16), vc)
            o_pair = o if g == 0 else jnp.where(half0, o_pair, o)
        ocmp_ref[0, :, r * LANES:(r + 1) * LANES] = gexp[:, r * LANES:(r + 1) * LANES] * o_pair

    n_sb = ovt_ref.shape[0]
    jb = lax.broadcasted_iota(jnp.int32, (n_sb, tq), 0)
    own = (t0 + lax.broadcasted_iota(jnp.int32, (n_sb, tq), 1)) // NSA_SEL_BLOCK
    forced = (jb == 0) | (jb == own) | (jb == own - 1)
    eye = _eye_bf16(tq)
    ovt = ovt_ref[...]
    for g in range(NSA_KV_HEADS):
        p_hi, p_lo = _split_bf16(psum[g])
        imp = _dot_nt(ovt, p_hi) + _dot_nt(ovt, p_lo)
        imp = jnp.where(jb <= own, imp + jnp.where(forced, NSA_FORCE_BONUS, 0.0), NEG_INF)
        counts = _rank_counts(imp, jb, n_sb)
        bias_t = jnp.where(counts < float(min(NSA_SEL_TOPN, n_sb)), 0.0, MASK_BIAS)
        selb_ref[0, :, g * LANES:(g + 1) * LANES] = _bias_to_rows(bias_t, eye)


def _nsa_cmp_call(qn, gn, kcmp, vcmp, ovt, eg_cmp):
    b, s, _ = qn.shape
    tq = min(TQ, s)
    n_cmp_pad = kcmp.shape[1]
    tok = lambda width: pl.BlockSpec((1, tq, width), lambda bi, i: (bi, i, 0))
    cspec = pl.BlockSpec((1, n_cmp_pad, LANES), lambda bi, i: (bi, 0, 0))
    return pl.pallas_call(
        _nsa_cmp_kernel,
        grid=(b, s // tq),
        in_specs=[tok(4 * LANES), tok(LANES), cspec, cspec, _const_spec(ovt.shape),
                  _const_spec(eg_cmp.shape)],
        out_specs=[tok(4 * LANES), tok(2 * LANES)],
        out_shape=[jax.ShapeDtypeStruct((b, s, 4 * LANES), F32),
                   jax.ShapeDtypeStruct((b, s, 2 * LANES), BF16)],
        compiler_params=_cparams("parallel", "parallel"),
        name="nsa_cmp",
    )(qn, gn, kcmp, vcmp, ovt, eg_cmp)


def _nsa_attn_kernel(q_ref, selb_ref, gate_ref, ocmp_ref, ksl_ref, vsl_ref, kwn_ref, vwn_ref,
                     oh_ref, egs_ref, egw_ref, o_ref):
    tq = q_ref.shape[1]
    i = pl.program_id(1)
    q2 = q_ref[0]
    half0 = _half_masks(tq)
    row = lax.broadcasted_iota(jnp.int32, (tq, tq), 0)
    col = lax.broadcasted_iota(jnp.int32, (tq, tq), 1)
    causal = col <= row
    zeros_aug = jnp.zeros((tq, LANES), BF16)

    o_sel = None
    o_win = None
    for g in range(NSA_KV_HEADS):
        qh = _take_half(q2, half0, g)
        q_aug = jnp.concatenate([qh, selb_ref[0, :, g * LANES:(g + 1) * LANES]], axis=1)

        def k_sel(c):
            return jnp.concatenate([_chunk(ksl_ref, c, tq), _onehot_chunk(oh_ref, c, tq)], axis=1)

        carry = _flash_first(q_aug, k_sel(i), _chunk(vsl_ref, i, tq), causal)
        carry = lax.fori_loop(
            0, i, lambda c, cr: _flash_next(q_aug, k_sel(c), _chunk(vsl_ref, c, tq), None, cr),
            carry)
        m, l, acc = carry
        os_g = acc / jnp.maximum(l, 1e-30)

        carry = _flash_first(qh, _chunk(kwn_ref, i, tq), _chunk(vwn_ref, i, tq), causal)

        def win_step(j, cr):
            c = i - 1 - j
            inside = (row - col) + (j + 1) * tq < NSA_WINDOW
            return _flash_next(qh, _chunk(kwn_ref, c, tq), _chunk(vwn_ref, c, tq), inside, cr)

        n_back = (NSA_WINDOW + tq - 1) // tq
        carry = lax.fori_loop(0, jnp.minimum(i, n_back), win_step, carry)
        m, l, acc = carry
        ow_g = acc / jnp.maximum(l, 1e-30)

        o_sel = os_g if g == 0 else jnp.where(half0, o_sel, os_g)
        o_win = ow_g if g == 0 else jnp.where(half0, o_win, ow_g)

    g_hi, g_lo = _split_bf16(gate_ref[0])
    gs = _dot(g_hi, egs_ref[...]) + _dot(g_lo, egs_ref[...])
    gw = _dot(g_hi, egw_ref[...]) + _dot(g_lo, egw_ref[...])
    o_ref[0] = (ocmp_ref[0] + gs * o_sel + gw * o_win).astype(BF16)


def _nsa_attn_call(qn, selb, gn, ocmp, ksl, vsl, kwn, vwn, onehot, eg_sel, eg_win):
    b, s, _ = qn.shape
    tq = min(TQ, s)
    n_pairs = qn.shape[2] // LANES
    pair = pl.BlockSpec((1, tq, LANES), lambda bi, i, r: (bi, i, r))
    tok = lambda width: pl.BlockSpec((1, tq, width), lambda bi, i, r: (bi, i, 0))
    seq = pl.BlockSpec((1, s, LANES), lambda bi, i, r: (bi, 0, 0))
    egspec = pl.BlockSpec((LANES, LANES), lambda bi, i, r: (0, r))
    return pl.pallas_call(
        _nsa_attn_kernel,
        grid=(b, s // tq, n_pairs),
        in_specs=[pair, tok(2 * LANES), tok(LANES), pair, seq, seq, seq, seq,
                  _const_spec(onehot.shape), egspec, egspec],
        out_specs=pair,
        out_shape=jax.ShapeDtypeStruct(qn.shape, BF16),
        compiler_params=_cparams("parallel", "parallel", "parallel"),
        name="nsa_attn",
    )(qn, selb, gn, ocmp, ksl, vsl, kwn, vwn, onehot, eg_sel, eg_win)


def _moba_kernel(q_ref, k_ref, v_ref, oh_ref, o_ref):
    tq = q_ref.shape[1]
    s = k_ref.shape[1]
    nb = s // tq
    i = pl.program_id(2)
    q2 = q_ref[0]
    half0 = _half_masks(tq)
    row = lax.broadcasted_iota(jnp.int32, (tq, tq), 0)
    col = lax.broadcasted_iota(jnp.int32, (tq, tq), 1)
    causal = col <= row
    eye = _eye_bf16(tq)

    jrow = lax.broadcasted_iota(jnp.int32, (nb, LANES), 0)
    k_mean = jnp.zeros((nb, LANES), F32)
    for j in range(nb):
        mean_j = jnp.mean(k_ref[0, j * tq:(j + 1) * tq, :].astype(F32), axis=0, keepdims=True)
        k_mean = jnp.where(jrow == j, mean_j, k_mean)
    km_hi, km_lo = _split_bf16(k_mean)
    jb = lax.broadcasted_iota(jnp.int32, (nb, tq), 0)
    n_top = min(MOBA_TOPK, nb - 1)

    out = None
    for h in range(2):
        qh = _take_half(q2, half0, h)
        gate = _dot_nt(km_hi, qh) + _dot_nt(km_lo, qh)
        gate = jnp.where(jb < i, gate, NEG_INF)
        counts = _rank_counts(gate, jb, nb)
        keep = jnp.where(jb < i, jnp.where(counts < float(n_top), 1.0, 0.0),
                         jnp.where(jb == i, 1.0, 0.0))
        bias_t = jnp.where(keep > 0.5, 0.0, MASK_BIAS)
        q_aug = jnp.concatenate([qh, _bias_to_rows(bias_t, eye)], axis=1)

        def k_aug(c):
            return jnp.concatenate([_chunk(k_ref, c, tq), _onehot_chunk(oh_ref, c, tq)], axis=1)

        carry = _flash_first(q_aug, k_aug(i), _chunk(v_ref, i, tq), causal)
        carry = lax.fori_loop(
            0, i, lambda c, cr: _flash_next(q_aug, k_aug(c), _chunk(v_ref, c, tq), None, cr),
            carry)
        m, l, acc = carry
        o_h = acc / jnp.maximum(l, 1e-30)
        out = o_h if h == 0 else jnp.where(half0, out, o_h)
    o_ref[0] = out.astype(BF16)


def _moba_call(qm, km, vm, onehot):
    b, s, w = qm.shape
    tq = min(MOBA_BLOCK, s)
    n_pairs = w // LANES
    pair = pl.BlockSpec((1, tq, LANES), lambda bi, p, i: (bi, i, p))
    seq = pl.BlockSpec((1, s, LANES), lambda bi, p, i: (bi, 0, p))
    return pl.pallas_call(
        _moba_kernel,
        grid=(b, n_pairs, s // tq),
        in_specs=[pair, seq, seq, _const_spec(onehot.shape)],
        out_specs=pair,
        out_shape=jax.ShapeDtypeStruct(qm.shape, BF16),
        compiler_params=_cparams("parallel", "parallel", "parallel"),
        name="moba",
    )(qm, km, vm, onehot)


def _merge_kernel(x_ref, sc_ref, sh_ref, gt_ref, g_ref, oa_ref, ob_ref, wga_ref, wgb_ref,
                  wbn_ref, wbm_ref, wo_ref, o_ref):
    x = x_ref[0]
    h = _adaln_norm(x, g_ref[...], sc_ref[...], sh_ref[...]).astype(BF16)
    ga = jax.nn.sigmoid(_dot(h, wga_ref[...]))
    gb = jax.nn.sigmoid(_dot(h, wgb_ref[...]))
    mixed = ga * _dot(oa_ref[0], wbn_ref[...]) + gb * _dot(ob_ref[0], wbm_ref[...])
    o_ref[0] = x + gt_ref[...] * _dot(mixed.astype(BF16), wo_ref[...])


def _merge_call(x, mod4, g_attn, oa, ob, wga, wgb, wbn, wbm, wo):
    b, s, d = x.shape
    tm = min(TM_PROJ, s)
    tok = lambda width: pl.BlockSpec((1, tm, width), lambda bi, i: (bi, i, 0))
    modspec = lambda k: pl.BlockSpec((None, None, 1, d), lambda bi, i: (bi, k, 0, 0))
    return pl.pallas_call(
        _merge_kernel,
        grid=(b, s // tm),
        in_specs=[tok(d), modspec(1), modspec(0), modspec(2), _const_spec((1, d)),
                  tok(oa.shape[2]), tok(ob.shape[2]), _const_spec(wga.shape),
                  _const_spec(wgb.shape), _const_spec(wbn.shape), _const_spec(wbm.shape),
                  _const_spec(wo.shape)],
        out_specs=tok(d),
        out_shape=jax.ShapeDtypeStruct(x.shape, F32),
        compiler_params=_cparams("parallel", "parallel"),
        name="merge",
    )(x, mod4, mod4, mod4, g_attn, oa, ob, wga, wgb, wbn, wbm, wo)


HALO = 8


def _ffn_kernel(x_ref, halo_ref, sc_ref, sh_ref, gt_ref, g_ref, wa_ref, wv_ref, cw_ref, cb_ref,
                wd_ref, o_ref, acc_ref, a_ref):
    x = x_ref[0]
    tm = x.shape[0]
    g = g_ref[...]
    h = _adaln_norm(x, g, sc_ref[...], sh_ref[...]).astype(BF16)
    h_halo = _adaln_norm(halo_ref[0], g, sc_ref[...], sh_ref[...]).astype(BF16)
    h_ext = jnp.concatenate([h_halo, h], axis=0)
    ext_row = lax.broadcasted_iota(jnp.int32, (HALO + tm, FF_CHUNK), 0)
    live = ext_row >= jnp.where(pl.program_id(1) > 0, 0, HALO)
    acc_ref[...] = jnp.zeros_like(acc_ref)

    def chunk(c, _):
        a_ref[...] = jnp.where(live, _dot(h_ext, wa_ref[c]), 0.0)
        cw = cw_ref[c]
        y = cb_ref[c]
        for k in range(CONV_WIDTH):
            lo = HALO - (CONV_WIDTH - 1) + k
            y = y + cw[k:k + 1, :] * a_ref[pl.ds(lo, tm), :]
        gated = jax.nn.gelu(y) * _dot(h, wv_ref[c])
        acc_ref[...] += _dot(gated.astype(BF16), wd_ref[c])
        return 0

    lax.fori_loop(0, wa_ref.shape[0], chunk, 0)
    o_ref[0] = x + gt_ref[...] * acc_ref[...]


def _ffn_call(x, mod4, g_ffn, wa, wv, cw, cb, wd):
    b, s, d = x.shape
    tm = min(TM_PROJ, s)
    tok = pl.BlockSpec((1, tm, d), lambda bi, i: (bi, i, 0))
    halo = pl.BlockSpec((1, HALO, d), lambda bi, i: (bi, jnp.maximum(i * (tm // HALO) - 1, 0), 0))
    modspec = lambda k: pl.BlockSpec((None, None, 1, d), lambda bi, i: (bi, k, 0, 0))
    return pl.pallas_call(
        _ffn_kernel,
        grid=(b, s // tm),
        in_specs=[tok, halo, modspec(4), modspec(3), modspec(5), _const_spec((1, d)),
                  _const_spec(wa.shape), _const_spec(wv.shape), _const_spec(cw.shape),
                  _const_spec(cb.shape), _const_spec(wd.shape)],
        out_specs=tok,
        out_shape=jax.ShapeDtypeStruct(x.shape, F32),
        scratch_shapes=[pltpu.VMEM((tm, d), F32), pltpu.VMEM((HALO + tm, FF_CHUNK), F32)],
        compiler_params=_cparams("parallel", "parallel"),
        name="ffn",
    )(x, x, mod4, mod4, mod4, g_ffn, wa, wv, cw, cb, wd)


def _block_diag_mean():
    lane = np.arange(LANES)
    return (lane[:, None] // HEAD_DIM == lane[None, :] // HEAD_DIM).astype(np.float32) / HEAD_DIM


def _overlap_t(n_cmp, n_cmp_pad, n_sb):
    c_start = np.arange(n_cmp)[:, None] * NSA_CMP_STRIDE
    js = np.arange(n_sb)[None, :]
    ov = ((c_start < (js + 1) * NSA_SEL_BLOCK) & (c_start + NSA_CMP_BLOCK > js * NSA_SEL_BLOCK))
    out = np.zeros((n_sb, n_cmp_pad), np.float32)
    out[:, :n_cmp] = ov.T
    return out


def _onehot_blocks(s, block):
    out = np.zeros((s, LANES), np.float32)
    out[np.arange(s), np.arange(s) // block] = 1.0
    return out


def _gate_expand(branch):
    out = np.zeros((LANES, NSA_WIDTH), np.float32)
    for slot, head in enumerate(NSA_HEAD_ORDER):
        out[3 * head + branch, slot * HEAD_DIM:(slot + 1) * HEAD_DIM] = 1.0
    return out


def _pad_cols(w, g):
    z = jnp.zeros_like(w)
    return jnp.concatenate([w, z] if g == 0 else [z, w], axis=1)


def _block_diag2(w):
    z = jnp.zeros_like(w)
    return jnp.concatenate([jnp.concatenate([w, z], axis=1), jnp.concatenate([z, w], axis=1)], axis=0)


def _layer(x, c, positions, w_ada, b_ada, g_attn_norm, w_in, g_q_nsa, g_k_cmp, g_k_slc, g_k_win,
           cmp_k_pos, cmp_k_w1, cmp_k_w2, cmp_v_pos, cmp_v_w1, cmp_v_w2, g_q_moba, g_k_moba,
           w_branch_nsa, w_branch_moba, w_out, g_ffn_norm, w_ffn_up, conv_w, conv_b, w_ffn_down):
    b, s, d = x.shape
    order = np.asarray(NSA_HEAD_ORDER)
    off = IN_OFFSETS

    mod = _mod_call(c, w_ada, b_ada)
    mod4 = mod.reshape(b, 6, 1, d)

    col = lambda k: w_in[:, off[k]:off[k + 1]]
    w_qn = col(0).reshape(d, NSA_HEADS, HEAD_DIM)[:, order].reshape(d, NSA_WIDTH)
    w_gn = jnp.pad(col(7), ((0, 0), (0, LANES - 3 * NSA_HEADS)))
    w_p = jnp.concatenate([w_qn, col(8), col(9), col(10), col(1), col(2), col(3), col(4), col(5),
                           col(6), w_gn], axis=1).astype(BF16)
    tile2 = lambda gv: jnp.tile(gv, LANES // HEAD_DIM)
    gains = jnp.stack([tile2(g_q_nsa), tile2(g_q_moba), tile2(g_k_moba), tile2(g_k_slc),
                       tile2(g_k_win)] + [jnp.ones((LANES,), F32)] * 3)
    bd = jnp.asarray(_block_diag_mean(), BF16)
    half = HEAD_DIM // 2
    inv_freq = ROPE_THETA ** (-jnp.arange(half, dtype=F32) / half)
    inv_freq = jnp.tile(inv_freq, LANES // half).reshape(1, LANES)

    (qn, qm, km, vm, kc, vc, ksl, vsl, kwn, vwn, gn) = _inproj_call(
        x, mod4, g_attn_norm.reshape(1, d), positions.reshape(b, s, 1), inv_freq, w_p, gains, bd)

    n_grp = s // NSA_CMP_STRIDE
    n_cmp = (s - NSA_CMP_BLOCK) // NSA_CMP_STRIDE + 1
    regroup = lambda t: t.reshape(b, n_grp, NSA_CMP_STRIDE, NSA_KV_HEADS, HEAD_DIM) \
        .transpose(0, 3, 1, 2, 4).reshape(b, NSA_KV_HEADS, n_grp, NSA_CMP_STRIDE * HEAD_DIM)
    halves = NSA_CMP_BLOCK // NSA_CMP_STRIDE
    w1_pack = lambda w1: jnp.stack([
        jnp.stack([_pad_cols(wh, g) for g in range(NSA_KV_HEADS)])
        for wh in jnp.split(w1, halves, axis=0)]).astype(BF16)
    pos_pack = lambda p: p.reshape(halves, NSA_CMP_STRIDE * HEAD_DIM)
    kcmp, vcmp = _compress_call(
        regroup(kc), regroup(vc), pos_pack(cmp_k_pos), pos_pack(cmp_v_pos),
        w1_pack(cmp_k_w1), w1_pack(cmp_v_w1), _block_diag2(cmp_k_w2).astype(BF16),
        _block_diag2(cmp_v_w2).astype(BF16), tile2(g_k_cmp).reshape(1, LANES), bd)

    n_sb = s // NSA_SEL_BLOCK
    ovt = jnp.asarray(_overlap_t(n_cmp, n_grp, n_sb), BF16)
    ocmp, selb = _nsa_cmp_call(qn, gn, kcmp, vcmp, ovt, jnp.asarray(_gate_expand(0), BF16))
    oa = _nsa_attn_call(qn, selb, gn, ocmp, ksl, vsl, kwn, vwn,
                        jnp.asarray(_onehot_blocks(s, NSA_SEL_BLOCK), BF16),
                        jnp.asarray(_gate_expand(1), BF16), jnp.asarray(_gate_expand(2), BF16))

    ob = _moba_call(qm, km, vm, jnp.asarray(_onehot_blocks(s, MOBA_BLOCK), BF16))

    w_bn = w_branch_nsa.reshape(NSA_HEADS, HEAD_DIM, d)[order].reshape(NSA_WIDTH, d)
    x1 = _merge_call(x, mod4, g_attn_norm.reshape(1, d), oa, ob, col(11).astype(BF16),
                     col(12).astype(BF16), w_bn.astype(BF16), w_branch_moba.astype(BF16),
                     w_out.astype(BF16))

    chunks = lambda w: w.reshape(w.shape[0], N_FF_CHUNKS, FF_CHUNK).transpose(1, 0, 2)
    wa = chunks(w_ffn_up[:, :D_FF]).astype(BF16)
    wv = chunks(w_ffn_up[:, D_FF:]).astype(BF16)
    cw = chunks(conv_w)
    cb = chunks(conv_b.reshape(1, D_FF))
    wd = w_ffn_down.reshape(N_FF_CHUNKS, FF_CHUNK, d).astype(BF16)
    return _ffn_call(x1, mod4, g_ffn_norm.reshape(1, d), wa, wv, cw, cb, wd)


def kernel(x, c, positions, w_ada, b_ada, g_attn_norm, w_in, g_q_nsa, g_k_cmp, g_k_slc, g_k_win,
           cmp_k_pos, cmp_k_w1, cmp_k_w2, cmp_v_pos, cmp_v_w1, cmp_v_w2, g_q_moba, g_k_moba,
           w_branch_nsa, w_branch_moba, w_out, g_ffn_norm, w_ffn_up, conv_w, conv_b, w_ffn_down):
    for l in range(w_ada.shape[0]):
        x = _layer(x, c, positions, w_ada[l], b_ada[l], g_attn_norm[l], w_in[l], g_q_nsa[l],
                   g_k_cmp[l], g_k_slc[l], g_k_win[l], cmp_k_pos[l], cmp_k_w1[l], cmp_k_w2[l],
                   cmp_v_pos[l], cmp_v_w1[l], cmp_v_w2[l], g_q_moba[l], g_k_moba[l],
                   w_branch_nsa[l], w_branch_moba[l], w_out[l], g_ffn_norm[l], w_ffn_up[l],
                   conv_w[l], conv_b[l], w_ffn_down[l])
    return x
```

```python
import functools

import numpy as np
import jax
import jax.numpy as jnp
from jax import lax
from jax.experimental import pallas as pl
from jax.experimental.pallas import tpu as pltpu

F32 = jnp.float32
BF16 = jnp.bfloat16

D_MODEL = 1024
HEAD_DIM = 64
NSA_HEADS = 8
NSA_KV_HEADS = 2
NSA_CMP_BLOCK = 32
NSA_CMP_STRIDE = 16
NSA_SEL_BLOCK = 64
NSA_SEL_TOPN = 16
NSA_WINDOW = 512
NSA_FORCE_BONUS = 1e4
MOBA_HEADS = 8
MOBA_BLOCK = 256
MOBA_TOPK = 3
D_FF = 2816
CONV_WIDTH = 3
ROPE_THETA = 10000.0
NORM_EPS = 1e-6
NEG_INF = -1e30

LANES = 128
SCALE = HEAD_DIM ** -0.5
MASK_BIAS = -(2.0 ** 100)
VMEM_LIMIT = 56 * 1024 * 1024

TM_PROJ = 512
TQ = 256
FF_CHUNK = 256
N_FF_CHUNKS = D_FF // FF_CHUNK

NSA_WIDTH = NSA_HEADS * HEAD_DIM
MOBA_WIDTH = MOBA_HEADS * HEAD_DIM
KV_WIDTH = NSA_KV_HEADS * HEAD_DIM
IN_SIZES = (NSA_WIDTH, KV_WIDTH, KV_WIDTH, KV_WIDTH, KV_WIDTH, KV_WIDTH, KV_WIDTH,
            3 * NSA_HEADS, MOBA_WIDTH, MOBA_WIDTH, MOBA_WIDTH, D_MODEL, D_MODEL)
IN_OFFSETS = np.concatenate([[0], np.cumsum(IN_SIZES)]).tolist()

NSA_HEAD_ORDER = (0, 4, 1, 5, 2, 6, 3, 7)


def _dot(a, b):
    return jnp.dot(a, b, preferred_element_type=F32)


def _dot_nt(a, b):
    return lax.dot_general(a, b, (((1,), (1,)), ((), ())), preferred_element_type=F32)


def _split_bf16(v):
    hi = v.astype(BF16)
    lo = (v - hi.astype(F32)).astype(BF16)
    return hi, lo


def _cparams(*sem):
    return pltpu.CompilerParams(dimension_semantics=sem, vmem_limit_bytes=VMEM_LIMIT)


def _const_spec(shape):
    n = len(shape)
    return pl.BlockSpec(shape, lambda *_: (0,) * n, pipeline_mode=pl.Buffered(1))


def _adaln_norm(x, g, sc, sh):
    y = x * lax.rsqrt(jnp.mean(x * x, axis=-1, keepdims=True) + NORM_EPS)
    return (y * g) * (1.0 + sc) + sh


def _mod_kernel(c_ref, w_ref, b_ref, o_ref):
    o_ref[...] = jnp.dot(c_ref[...], w_ref[...], preferred_element_type=F32,
                         precision=lax.Precision.HIGHEST) + b_ref[...]


def _mod_call(c, w_ada, b_ada):
    b, d = c.shape
    n = w_ada.shape[1]
    tn = D_MODEL
    return pl.pallas_call(
        _mod_kernel,
        grid=(n // tn,),
        in_specs=[pl.BlockSpec((b, d), lambda j: (0, 0)),
                  pl.BlockSpec((d, tn), lambda j: (0, j)),
                  pl.BlockSpec((1, tn), lambda j: (0, j))],
        out_specs=pl.BlockSpec((b, tn), lambda j: (0, j)),
        out_shape=jax.ShapeDtypeStruct((b, n), F32),
        compiler_params=_cparams("parallel"),
        name="mod",
    )(c, w_ada, b_ada.reshape(1, n))


_P_QN, _P_QM, _P_KM, _P_VM = 0, 512, 1024, 1536
_P_KC, _P_VC, _P_KSL, _P_VSL, _P_KWN, _P_VWN, _P_GN = 2048, 2176, 2304, 2432, 2560, 2688, 2816
_P_WIDTH = 2944


def _inproj_kernel(x_ref, sc_ref, sh_ref, g_ref, pos_ref, invf_ref, w_ref, gains_ref, bd_ref,
                   qn_ref, qm_ref, km_ref, vm_ref, kc_ref, vc_ref, ksl_ref, vsl_ref, kwn_ref,
                   vwn_ref, gn_ref):
    x = x_ref[0]
    tm = x.shape[0]
    h = _adaln_norm(x, g_ref[...], sc_ref[...], sh_ref[...]).astype(BF16)

    ang = pos_ref[0].astype(F32) * invf_ref[...]
    cos = jnp.cos(ang)
    sin = jnp.sin(ang)
    lane = lax.broadcasted_iota(jnp.int32, (tm, LANES), 1)
    first = (lane & (HEAD_DIM // 2)) == 0
    sin_signed = jnp.where(first, -sin, sin)
    bd = bd_ref[...]

    def head_norm(y, gain):
        ms = _dot((y * y).astype(BF16), bd)
        return y * lax.rsqrt(ms + NORM_EPS) * gain

    def rope(y):
        partner = jnp.where(first, pltpu.roll(y, LANES - HEAD_DIM // 2, 1),
                            pltpu.roll(y, HEAD_DIM // 2, 1))
        return y * cos + partner * sin_signed

    def proj(off, width):
        return _dot(h, w_ref[:, off:off + width])

    def wide(off, out_ref, gain_row, scale):
        acc = proj(off, 4 * LANES)
        gain = gains_ref[gain_row:gain_row + 1, :]
        for p in range(4):
            y = rope(head_norm(acc[:, p * LANES:(p + 1) * LANES], gain))
            if scale != 1.0:
                y = y * scale
            out_ref[0, :, p * LANES:(p + 1) * LANES] = y.astype(BF16)

    wide(_P_QN, qn_ref, 0, SCALE)
    wide(_P_QM, qm_ref, 1, SCALE)
    wide(_P_KM, km_ref, 2, 1.0)
    vm_ref[0] = proj(_P_VM, 4 * LANES).astype(BF16)

    small = proj(_P_KC, 7 * LANES)
    kc_ref[0] = rope(small[:, 0:LANES]).astype(BF16)
    vc_ref[0] = small[:, LANES:2 * LANES].astype(BF16)
    ksl_ref[0] = rope(head_norm(small[:, 2 * LANES:3 * LANES], gains_ref[3:4, :])).astype(BF16)
    vsl_ref[0] = small[:, 3 * LANES:4 * LANES].astype(BF16)
    kwn_ref[0] = rope(head_norm(small[:, 4 * LANES:5 * LANES], gains_ref[4:5, :])).astype(BF16)
    vwn_ref[0] = small[:, 5 * LANES:6 * LANES].astype(BF16)
    gn_ref[0] = jax.nn.sigmoid(small[:, 6 * LANES:7 * LANES])


def _inproj_call(x, mod4, g_attn, pos3, inv_freq, w_p, gains, bd):
    b, s, d = x.shape
    tm = min(TM_PROJ, s)
    tok = lambda width: pl.BlockSpec((1, tm, width), lambda bi, i: (bi, i, 0))
    modspec = lambda k: pl.BlockSpec((None, None, 1, d), lambda bi, i: (bi, k, 0, 0))
    shapes = [jax.ShapeDtypeStruct((b, s, 4 * LANES), BF16)] * 4 \
        + [jax.ShapeDtypeStruct((b, s, LANES), BF16)] * 6 \
        + [jax.ShapeDtypeStruct((b, s, LANES), F32)]
    return pl.pallas_call(
        _inproj_kernel,
        grid=(b, s // tm),
        in_specs=[tok(d), modspec(1), modspec(0), _const_spec((1, d)), tok(1),
                  _const_spec((1, LANES)), _const_spec((d, _P_WIDTH)),
                  _const_spec((8, LANES)), _const_spec((LANES, LANES))],
        out_specs=[tok(4 * LANES)] * 4 + [tok(LANES)] * 7,
        out_shape=shapes,
        compiler_params=_cparams("parallel", "parallel"),
        name="inproj",
    )(x, mod4, mod4, g_attn, pos3, inv_freq, w_p, gains, bd)


def _compress_kernel(xk_ref, xv_ref, posk_ref, posv_ref, w1k_ref, w1v_ref, w2k_ref, w2v_ref,
                     gain_ref, bd_ref, ko_ref, vo_ref):
    def mlp(x_ref, pos_ref, w1_ref, w2_ref):
        acc = None
        for g in range(NSA_KV_HEADS):
            xg = x_ref[0, g].astype(F32)
            xa = (xg + pos_ref[0:1, :]).astype(BF16)
            xb = (xg + pos_ref[1:2, :]).astype(BF16)
            a = _dot(xa, w1_ref[0, g])
            bm = _dot(xb, w1_ref[1, g])
            part = a + pltpu.roll(bm, bm.shape[0] - 1, 0)
            acc = part if acc is None else acc + part
        return _dot(jax.nn.gelu(acc).astype(BF16), w2_ref[...])

    k = mlp(xk_ref, posk_ref, w1k_ref, w2k_ref)
    ms = _dot((k * k).astype(BF16), bd_ref[...])
    ko_ref[0] = (k * lax.rsqrt(ms + NORM_EPS) * gain_ref[...]).astype(BF16)
    vo_ref[0] = mlp(xv_ref, posv_ref, w1v_ref, w2v_ref).astype(BF16)


def _compress_call(xk, xv, posk, posv, w1k, w1v, w2k, w2v, gain, bd):
    b, g, n, w = xk.shape
    xspec = pl.BlockSpec((1, g, n, w), lambda bi: (bi, 0, 0, 0))
    ospec = pl.BlockSpec((1, n, LANES), lambda bi: (bi, 0, 0))
    return pl.pallas_call(
        _compress_kernel,
        grid=(b,),
        in_specs=[xspec, xspec, _const_spec(posk.shape), _const_spec(posv.shape),
                  _const_spec(w1k.shape), _const_spec(w1v.shape), _const_spec(w2k.shape),
                  _const_spec(w2v.shape), _const_spec((1, LANES)), _const_spec((LANES, LANES))],
        out_specs=[ospec, ospec],
        out_shape=[jax.ShapeDtypeStruct((b, n, LANES), BF16)] * 2,
        compiler_params=_cparams("parallel"),
        name="compress",
    )(xk, xv, posk, posv, w1k, w1v, w2k, w2v, gain, bd)


def _half_masks(tq):
    lane = lax.broadcasted_iota(jnp.int32, (tq, LANES), 1)
    return lane < HEAD_DIM


def _take_half(q2, half0, g):
    qf = q2.astype(F32)
    keep = half0 if g == 0 else jnp.logical_not(half0)
    return jnp.where(keep, qf, 0.0).astype(BF16)


def _eye_bf16(n):
    r = lax.broadcasted_iota(jnp.int32, (n, n), 0)
    c = lax.broadcasted_iota(jnp.int32, (n, n), 1)
    return jnp.where(r == c, 1.0, 0.0).astype(BF16)


def _rank_counts(vals, jb, n):
    counts = jnp.zeros_like(vals)
    for j in range(n):
        row = vals[j:j + 1, :]
        beats = jnp.where(vals > row, 1.0,
                          jnp.where(vals == row, jnp.where(jb < j, 1.0, 0.0), 0.0))
        cnt = jnp.sum(beats, axis=0, keepdims=True)
        counts = jnp.where(jb == j, cnt, counts)
    return counts


def _bias_to_rows(bias_t, eye):
    n, tq = bias_t.shape
    padded = jnp.concatenate([bias_t, jnp.zeros((LANES - n, tq), F32)], axis=0).astype(BF16)
    return _dot_nt(eye, padded).astype(BF16)


def _softmax_attend(parts):
    m = None
    for s, _ in parts:
        mx = jnp.max(s, axis=-1, keepdims=True)
        m = mx if m is None else jnp.maximum(m, mx)
    l = None
    acc = None
    for s, v in parts:
        p = jnp.exp(s - m)
        ls = jnp.sum(p, axis=-1, keepdims=True)
        pv = _dot(p.astype(BF16), v)
        l = ls if l is None else l + ls
        acc = pv if acc is None else acc + pv
    return acc / jnp.maximum(l, 1e-30)


def _with_onehot(k_ref, oh_ref, lo, hi):
    return jnp.concatenate([k_ref[0, lo:hi, :], oh_ref[lo:hi, :]], axis=1)


def _nsa_cmp_kernel(q_ref, gate_ref, kc_ref, vc_ref, ovt_ref, eg_ref, ocmp_ref, selb_ref):
    tq = q_ref.shape[1]
    t0 = pl.program_id(1) * tq
    g_hi, g_lo = _split_bf16(gate_ref[0])
    gexp = _dot(g_hi, eg_ref[...]) + _dot(g_lo, eg_ref[...])

    lane = lax.broadcasted_iota(jnp.int32, (tq, LANES), 1)
    row_t = t0 + lax.broadcasted_iota(jnp.int32, (tq, LANES), 0)
    visible = lane * NSA_CMP_STRIDE + (NSA_CMP_BLOCK - 1) <= row_t
    half0 = lane < HEAD_DIM
    kc = kc_ref[0]
    vc = vc_ref[0]

    psum = [None, None]
    for r in range(NSA_HEADS // NSA_KV_HEADS):
        q2 = q_ref[0, :, r * LANES:(r + 1) * LANES]
        o_pair = None
        for g in range(NSA_KV_HEADS):
            s = _dot_nt(_take_half(q2, half0, g), kc)
            s = jnp.where(visible, s, NEG_INF)
            m = jnp.max(s, axis=-1, keepdims=True)
            p = jnp.where(visible, jnp.exp(s - m), 0.0)
            p = p / jnp.maximum(jnp.sum(p, axis=-1, keepdims=True), 1e-30)
            psum[g] = p if psum[g] is None else psum[g] + p
            o = _dot(p.astype(BF16), vc)
            o_pair = o if g == 0 else jnp.where(half0, o_pair, o)
        ocmp_ref[0, :, r * LANES:(r + 1) * LANES] = gexp[:, r * LANES:(r + 1) * LANES] * o_pair

    n_sb = ovt_ref.shape[0]
    jb = lax.broadcasted_iota(jnp.int32, (n_sb, tq), 0)
    own = (t0 + lax.broadcasted_iota(jnp.int32, (n_sb, tq), 1)) // NSA_SEL_BLOCK
    forced = (jb == 0) | (jb == own) | (jb == own - 1)
    eye = _eye_bf16(tq)
    ovt = ovt_ref[...]
    for g in range(NSA_KV_HEADS):
        p_hi, p_lo = _split_bf16(psum[g])
        imp = _dot_nt(ovt, p_hi) + _dot_nt(ovt, p_lo)
        imp = jnp.where(jb <= own, imp + jnp.where(forced, NSA_FORCE_BONUS, 0.0), NEG_INF)
        counts = _rank_counts(imp, jb, n_sb)
        bias_t = jnp.where(counts < float(min(NSA_SEL_TOPN, n_sb)), 0.0, MASK_BIAS)
        selb_ref[0, :, g * LANES:(g + 1) * LANES] = _bias_to_rows(bias_t, eye)


def _nsa_cmp_call(qn, gn, kcmp, vcmp, ovt, eg_cmp):
    b, s, _ = qn.shape
    tq = min(TQ, s)
    n_cmp_pad = kcmp.shape[1]
    tok = lambda width: pl.BlockSpec((1, tq, width), lambda bi, i: (bi, i, 0))
    cspec = pl.BlockSpec((1, n_cmp_pad, LANES), lambda bi, i: (bi, 0, 0))
    return pl.pallas_call(
        _nsa_cmp_kernel,
        grid=(b, s // tq),
        in_specs=[tok(4 * LANES), tok(LANES), cspec, cspec, _const_spec(ovt.shape),
                  _const_spec(eg_cmp.shape)],
        out_specs=[tok(4 * LANES), tok(2 * LANES)],
        out_shape=[jax.ShapeDtypeStruct((b, s, 4 * LANES), F32),
                   jax.ShapeDtypeStruct((b, s, 2 * LANES), BF16)],
        compiler_params=_cparams("parallel", "parallel"),
        name="nsa_cmp",
    )(qn, gn, kcmp, vcmp, ovt, eg_cmp)


def _nsa_attn_kernel(q_ref, selb_ref, gate_ref, ocmp_ref, ksl_ref, vsl_ref, kwn_ref, vwn_ref,
                     oh_ref, egs_ref, egw_ref, o_ref):
    tq = q_ref.shape[1]
    n_tiles = ksl_ref.shape[1] // tq
    i = pl.program_id(1)
    q2 = q_ref[0]
    half0 = _half_masks(tq)
    row = lax.broadcasted_iota(jnp.int32, (tq, tq), 0)
    col = lax.broadcasted_iota(jnp.int32, (tq, tq), 1)
    causal = col <= row
    qh = [_take_half(q2, half0, g) for g in range(NSA_KV_HEADS)]
    q_aug = [jnp.concatenate([qh[g], selb_ref[0, :, g * LANES:(g + 1) * LANES]], axis=1)
             for g in range(NSA_KV_HEADS)]
    g_hi, g_lo = _split_bf16(gate_ref[0])
    gs = _dot(g_hi, egs_ref[...]) + _dot(g_lo, egs_ref[...])
    gw = _dot(g_hi, egw_ref[...]) + _dot(g_lo, egw_ref[...])

    def tile_case(n):
        lo = (n - 1) * tq
        o_sel = None
        o_win = None
        for g in range(NSA_KV_HEADS):
            parts = []
            if n > 1:
                parts.append((_dot_nt(q_aug[g], _with_onehot(ksl_ref, oh_ref, 0, lo)),
                              vsl_ref[0, 0:lo, :]))
            s_diag = _dot_nt(q_aug[g], _with_onehot(ksl_ref, oh_ref, lo, lo + tq))
            parts.append((jnp.where(causal, s_diag, NEG_INF), vsl_ref[0, lo:lo + tq, :]))
            os_g = _softmax_attend(parts)

            parts = [(jnp.where(causal, _dot_nt(qh[g], kwn_ref[0, lo:lo + tq, :]), NEG_INF),
                      vwn_ref[0, lo:lo + tq, :])]
            for back in range(1, min(n - 1, (NSA_WINDOW + tq - 2) // tq) + 1):
                k0 = lo - back * tq
                sw = _dot_nt(qh[g], kwn_ref[0, k0:k0 + tq, :])
                if (back + 1) * tq - 1 >= NSA_WINDOW:
                    sw = jnp.where(row - col + back * tq < NSA_WINDOW, sw, NEG_INF)
                parts.append((sw, vwn_ref[0, k0:k0 + tq, :]))
            ow_g = _softmax_attend(parts)

            o_sel = os_g if g == 0 else jnp.where(half0, o_sel, os_g)
            o_win = ow_g if g == 0 else jnp.where(half0, o_win, ow_g)
        o_ref[0] = (ocmp_ref[0] + gs * o_sel + gw * o_win).astype(BF16)

    for n in range(1, n_tiles + 1):
        pl.when(i == n - 1)(functools.partial(tile_case, n))


def _nsa_attn_call(qn, selb, gn, ocmp, ksl, vsl, kwn, vwn, onehot, eg_sel, eg_win):
    b, s, _ = qn.shape
    tq = min(TQ, s)
    n_pairs = qn.shape[2] // LANES
    pair = pl.BlockSpec((1, tq, LANES), lambda bi, i, r: (bi, i, r))
    tok = lambda width: pl.BlockSpec((1, tq, width), lambda bi, i, r: (bi, i, 0))
    seq = pl.BlockSpec((1, s, LANES), lambda bi, i, r: (bi, 0, 0))
    egspec = pl.BlockSpec((LANES, LANES), lambda bi, i, r: (0, r))
    return pl.pallas_call(
        _nsa_attn_kernel,
        grid=(b, s // tq, n_pairs),
        in_specs=[pair, tok(2 * LANES), tok(LANES), pair, seq, seq, seq, seq,
                  _const_spec(onehot.shape), egspec, egspec],
        out_specs=pair,
        out_shape=jax.ShapeDtypeStruct(qn.shape, BF16),
        compiler_params=_cparams("parallel", "parallel", "parallel"),
        name="nsa_attn",
    )(qn, selb, gn, ocmp, ksl, vsl, kwn, vwn, onehot, eg_sel, eg_win)


def _moba_kernel(q_ref, k_ref, v_ref, oh_ref, o_ref, kmean_ref):
    tq = q_ref.shape[1]
    nb = k_ref.shape[1] // tq
    i = pl.program_id(2)
    q2 = q_ref[0]
    half0 = _half_masks(tq)
    row = lax.broadcasted_iota(jnp.int32, (tq, tq), 0)
    col = lax.broadcasted_iota(jnp.int32, (tq, tq), 1)
    causal = col <= row
    eye = _eye_bf16(tq)

    @pl.when(i == 0)
    def _():
        for j in range(nb):
            kmean_ref[j:j + 1, :] = jnp.mean(k_ref[0, j * tq:(j + 1) * tq, :].astype(F32), axis=0,
                                              keepdims=True)

    km_hi, km_lo = _split_bf16(kmean_ref[...])
    jb = lax.broadcasted_iota(jnp.int32, (nb, tq), 0)
    n_top = min(MOBA_TOPK, nb - 1)

    qh = []
    q_aug = []
    for h in range(2):
        qh.append(_take_half(q2, half0, h))
        gate = _dot_nt(km_hi, qh[h]) + _dot_nt(km_lo, qh[h])
        gate = jnp.where(jb < i, gate, NEG_INF)
        counts = _rank_counts(gate, jb, nb)
        bias_t = jnp.where(jb < i, jnp.where(counts < float(n_top), 0.0, MASK_BIAS), 0.0)
        q_aug.append(jnp.concatenate([qh[h], _bias_to_rows(bias_t, eye)], axis=1))

    def tile_case(n):
        lo = (n - 1) * tq
        outs = []
        for h in range(2):
            parts = []
            if n > 1:
                parts.append((_dot_nt(q_aug[h], _with_onehot(k_ref, oh_ref, 0, lo)),
                              v_ref[0, 0:lo, :]))
            s_own = _dot_nt(qh[h], k_ref[0, lo:lo + tq, :])
            parts.append((jnp.where(causal, s_own, NEG_INF), v_ref[0, lo:lo + tq, :]))
            outs.append(_softmax_attend(parts))
        o_ref[0] = jnp.where(half0, outs[0], outs[1]).astype(BF16)

    for n in range(1, nb + 1):
        pl.when(i == n - 1)(functools.partial(tile_case, n))


def _moba_call(qm, km, vm, onehot):
    b, s, w = qm.shape
    tq = min(MOBA_BLOCK, s)
    n_pairs = w // LANES
    pair = pl.BlockSpec((1, tq, LANES), lambda bi, p, i: (bi, i, p))
    seq = pl.BlockSpec((1, s, LANES), lambda bi, p, i: (bi, 0, p))
    return pl.pallas_call(
        _moba_kernel,
        grid=(b, n_pairs, s // tq),
        in_specs=[pair, seq, seq, _const_spec(onehot.shape)],
        out_specs=pair,
        out_shape=jax.ShapeDtypeStruct(qm.shape, BF16),
        scratch_shapes=[pltpu.VMEM((s // tq, LANES), F32)],
        compiler_params=_cparams("parallel", "parallel", "arbitrary"),
        name="moba",
    )(qm, km, vm, onehot)


def _merge_kernel(x_ref, sc_ref, sh_ref, gt_ref, g_ref, oa_ref, ob_ref, wga_ref, wgb_ref,
                  wbn_ref, wbm_ref, wo_ref, o_ref):
    x = x_ref[0]
    h = _adaln_norm(x, g_ref[...], sc_ref[...], sh_ref[...]).astype(BF16)
    ga = jax.nn.sigmoid(_dot(h, wga_ref[...]))
    gb = jax.nn.sigmoid(_dot(h, wgb_ref[...]))
    mixed = ga * _dot(oa_ref[0], wbn_ref[...]) + gb * _dot(ob_ref[0], wbm_ref[...])
    o_ref[0] = x + gt_ref[...] * _dot(mixed.astype(BF16), wo_ref[...])


def _merge_call(x, mod4, g_attn, oa, ob, wga, wgb, wbn, wbm, wo):
    b, s, d = x.shape
    tm = min(TM_PROJ, s)
    tok = lambda width: pl.BlockSpec((1, tm, width), lambda bi, i: (bi, i, 0))
    modspec = lambda k: pl.BlockSpec((None, None, 1, d), lambda bi, i: (bi, k, 0, 0))
    return pl.pallas_call(
        _merge_kernel,
        grid=(b, s // tm),
        in_specs=[tok(d), modspec(1), modspec(0), modspec(2), _const_spec((1, d)),
                  tok(oa.shape[2]), tok(ob.shape[2]), _const_spec(wga.shape),
                  _const_spec(wgb.shape), _const_spec(wbn.shape), _const_spec(wbm.shape),
                  _const_spec(wo.shape)],
        out_specs=tok(d),
        out_shape=jax.ShapeDtypeStruct(x.shape, F32),
        compiler_params=_cparams("parallel", "parallel"),
        name="merge",
    )(x, mod4, mod4, mod4, g_attn, oa, ob, wga, wgb, wbn, wbm, wo)


HALO = 8


def _ffn_kernel(x_ref, halo_ref, sc_ref, sh_ref, gt_ref, g_ref, wa_ref, wv_ref, cw_ref, cb_ref,
                wd_ref, o_ref, acc_ref, a_ref):
    x = x_ref[0]
    tm = x.shape[0]
    g = g_ref[...]
    h = _adaln_norm(x, g, sc_ref[...], sh_ref[...]).astype(BF16)
    h_halo = _adaln_norm(halo_ref[0], g, sc_ref[...], sh_ref[...]).astype(BF16)
    h_ext = jnp.concatenate([h_halo, h], axis=0)
    ext_row = lax.broadcasted_iota(jnp.int32, (HALO + tm, FF_CHUNK), 0)
    live = ext_row >= jnp.where(pl.program_id(1) > 0, 0, HALO)
    acc_ref[...] = jnp.zeros_like(acc_ref)

    def chunk(c, _):
        a_ref[...] = jnp.where(live, _dot(h_ext, wa_ref[c]), 0.0)
        cw = cw_ref[c]
        y = cb_ref[c]
        for k in range(CONV_WIDTH):
            lo = HALO - (CONV_WIDTH - 1) + k
            y = y + cw[k:k + 1, :] * a_ref[pl.ds(lo, tm), :]
        gated = jax.nn.gelu(y) * _dot(h, wv_ref[c])
        acc_ref[...] += _dot(gated.astype(BF16), wd_ref[c])
        return 0

    lax.fori_loop(0, wa_ref.shape[0], chunk, 0)
    o_ref[0] = x + gt_ref[...] * acc_ref[...]


def _ffn_call(x, mod4, g_ffn, wa, wv, cw, cb, wd):
    b, s, d = x.shape
    tm = min(TM_PROJ, s)
    tok = pl.BlockSpec((1, tm, d), lambda bi, i: (bi, i, 0))
    halo = pl.BlockSpec((1, HALO, d), lambda bi, i: (bi, jnp.maximum(i * (tm // HALO) - 1, 0), 0))
    modspec = lambda k: pl.BlockSpec((None, None, 1, d), lambda bi, i: (bi, k, 0, 0))
    return pl.pallas_call(
        _ffn_kernel,
        grid=(b, s // tm),
        in_specs=[tok, halo, modspec(4), modspec(3), modspec(5), _const_spec((1, d)),
                  _const_spec(wa.shape), _const_spec(wv.shape), _const_spec(cw.shape),
                  _const_spec(cb.shape), _const_spec(wd.shape)],
        out_specs=tok,
        out_shape=jax.ShapeDtypeStruct(x.shape, F32),
        scratch_shapes=[pltpu.VMEM((tm, d), F32), pltpu.VMEM((HALO + tm, FF_CHUNK), F32)],
        compiler_params=_cparams("parallel", "parallel"),
        name="ffn",
    )(x, x, mod4, mod4, mod4, g_ffn, wa, wv, cw, cb, wd)


def _block_diag_mean():
    lane = np.arange(LANES)
    return (lane[:, None] // HEAD_DIM == lane[None, :] // HEAD_DIM).astype(np.float32) / HEAD_DIM


def _overlap_t(n_cmp, n_cmp_pad, n_sb):
    c_start = np.arange(n_cmp)[:, None] * NSA_CMP_STRIDE
    js = np.arange(n_sb)[None, :]
    ov = ((c_start < (js + 1) * NSA_SEL_BLOCK) & (c_start + NSA_CMP_BLOCK > js * NSA_SEL_BLOCK))
    out = np.zeros((n_sb, n_cmp_pad), np.float32)
    out[:, :n_cmp] = ov.T
    return out


def _onehot_blocks(s, block):
    out = np.zeros((s, LANES), np.float32)
    out[np.arange(s), np.arange(s) // block] = 1.0
    return out


def _gate_expand(branch):
    out = np.zeros((LANES, NSA_WIDTH), np.float32)
    for slot, head in enumerate(NSA_HEAD_ORDER):
        out[3 * head + branch, slot * HEAD_DIM:(slot + 1) * HEAD_DIM] = 1.0
    return out


def _pad_cols(w, g):
    z = jnp.zeros_like(w)
    return jnp.concatenate([w, z] if g == 0 else [z, w], axis=1)


def _block_diag2(w):
    z = jnp.zeros_like(w)
    return jnp.concatenate([jnp.concatenate([w, z], axis=1), jnp.concatenate([z, w], axis=1)], axis=0)


def _layer(x, c, positions, w_ada, b_ada, g_attn_norm, w_in, g_q_nsa, g_k_cmp, g_k_slc, g_k_win,
           cmp_k_pos, cmp_k_w1, cmp_k_w2, cmp_v_pos, cmp_v_w1, cmp_v_w2, g_q_moba, g_k_moba,
           w_branch_nsa, w_branch_moba, w_out, g_ffn_norm, w_ffn_up, conv_w, conv_b, w_ffn_down):
    b, s, d = x.shape
    order = np.asarray(NSA_HEAD_ORDER)
    off = IN_OFFSETS

    mod = _mod_call(c, w_ada, b_ada)
    mod4 = mod.reshape(b, 6, 1, d)

    col = lambda k: w_in[:, off[k]:off[k + 1]]
    w_qn = col(0).reshape(d, NSA_HEADS, HEAD_DIM)[:, order].reshape(d, NSA_WIDTH)
    w_gn = jnp.pad(col(7), ((0, 0), (0, LANES - 3 * NSA_HEADS)))
    w_p = jnp.concatenate([w_qn, col(8), col(9), col(10), col(1), col(2), col(3), col(4), col(5),
                           col(6), w_gn], axis=1).astype(BF16)
    tile2 = lambda gv: jnp.tile(gv, LANES // HEAD_DIM)
    gains = jnp.stack([tile2(g_q_nsa), tile2(g_q_moba), tile2(g_k_moba), tile2(g_k_slc),
                       tile2(g_k_win)] + [jnp.ones((LANES,), F32)] * 3)
    bd = jnp.asarray(_block_diag_mean(), BF16)
    half = HEAD_DIM // 2
    inv_freq = ROPE_THETA ** (-jnp.arange(half, dtype=F32) / half)
    inv_freq = jnp.tile(inv_freq, LANES // half).reshape(1, LANES)

    (qn, qm, km, vm, kc, vc, ksl, vsl, kwn, vwn, gn) = _inproj_call(
        x, mod4, g_attn_norm.reshape(1, d), positions.reshape(b, s, 1), inv_freq, w_p, gains, bd)

    n_grp = s // NSA_CMP_STRIDE
    n_cmp = (s - NSA_CMP_BLOCK) // NSA_CMP_STRIDE + 1
    regroup = lambda t: t.reshape(b, n_grp, NSA_CMP_STRIDE, NSA_KV_HEADS, HEAD_DIM) \
        .transpose(0, 3, 1, 2, 4).reshape(b, NSA_KV_HEADS, n_grp, NSA_CMP_STRIDE * HEAD_DIM)
    halves = NSA_CMP_BLOCK // NSA_CMP_STRIDE
    w1_pack = lambda w1: jnp.stack([
        jnp.stack([_pad_cols(wh, g) for g in range(NSA_KV_HEADS)])
        for wh in jnp.split(w1, halves, axis=0)]).astype(BF16)
    pos_pack = lambda p: p.reshape(halves, NSA_CMP_STRIDE * HEAD_DIM)
    kcmp, vcmp = _compress_call(
        regroup(kc), regroup(vc), pos_pack(cmp_k_pos), pos_pack(cmp_v_pos),
        w1_pack(cmp_k_w1), w1_pack(cmp_v_w1), _block_diag2(cmp_k_w2).astype(BF16),
        _block_diag2(cmp_v_w2).astype(BF16), tile2(g_k_cmp).reshape(1, LANES), bd)

    n_sb = s // NSA_SEL_BLOCK
    ovt = jnp.asarray(_overlap_t(n_cmp, n_grp, n_sb), BF16)
    ocmp, selb = _nsa_cmp_call(qn, gn, kcmp, vcmp, ovt, jnp.asarray(_gate_expand(0), BF16))
    oa = _nsa_attn_call(qn, selb, gn, ocmp, ksl, vsl, kwn, vwn,
                        jnp.asarray(_onehot_blocks(s, NSA_SEL_BLOCK), BF16),
                        jnp.asarray(_gate_expand(1), BF16), jnp.asarray(_gate_expand(2), BF16))

    ob = _moba_call(qm, km, vm, jnp.asarray(_onehot_blocks(s, MOBA_BLOCK), BF16))

    w_bn = w_branch_nsa.reshape(NSA_HEADS, HEAD_DIM, d)[order].reshape(NSA_WIDTH, d)
    x1 = _merge_call(x, mod4, g_attn_norm.reshape(1, d), oa, ob, col(11).astype(BF16),
                     col(12).astype(BF16), w_bn.astype(BF16), w_branch_moba.astype(BF16),
                     w_out.astype(BF16))

    chunks = lambda w: w.reshape(w.shape[0], N_FF_CHUNKS, FF_CHUNK).transpose(1, 0, 2)
    wa = chunks(w_ffn_up[:, :D_FF]).astype(BF16)
    wv = chunks(w_ffn_up[:, D_FF:]).astype(BF16)
    cw = chunks(conv_w)
    cb = chunks(conv_b.reshape(1, D_FF))
    wd = w_ffn_down.reshape(N_FF_CHUNKS, FF_CHUNK, d).astype(BF16)
    return _ffn_call(x1, mod4, g_ffn_norm.reshape(1, d), wa, wv, cw, cb, wd)


def kernel(x, c, positions, w_ada, b_ada, g_attn_norm, w_in, g_q_nsa, g_k_cmp, g_k_slc, g_k_win,
           cmp_k_pos, cmp_k_w1, cmp_k_w2, cmp_v_pos, cmp_v_w1, cmp_v_w2, g_q_moba, g_k_moba,
           w_branch_nsa, w_branch_moba, w_out, g_ffn_norm, w_ffn_up, conv_w, conv_b, w_ffn_down):
    for l in range(w_ada.shape[0]):
        x = _layer(x, c, positions, w_ada[l], b_ada[l], g_attn_norm[l], w_in[l], g_q_nsa[l],
                   g_k_cmp[l], g_k_slc[l], g_k_win[l], cmp_k_pos[l], cmp_k_w1[l], cmp_k_w2[l],
                   cmp_v_pos[l], cmp_v_w1[l], cmp_v_w2[l], g_q_moba[l], g_k_moba[l],
                   w_branch_nsa[l], w_branch_moba[l], w_out[l], g_ffn_norm[l], w_ffn_up[l],
                   conv_w[l], conv_b[l], w_ffn_down[l])
    return x
```

```python
import numpy as np
import jax
import jax.numpy as jnp
from jax import lax
from jax.experimental import pallas as pl
from jax.experimental.pallas import tpu as pltpu

F32 = jnp.float32
BF16 = jnp.bfloat16

D_MODEL = 1024
HEAD_DIM = 64
NSA_HEADS = 8
NSA_KV_HEADS = 2
NSA_CMP_BLOCK = 32
NSA_CMP_STRIDE = 16
NSA_SEL_BLOCK = 64
NSA_SEL_TOPN = 16
NSA_WINDOW = 512
NSA_FORCE_BONUS = 1e4
MOBA_HEADS = 8
MOBA_BLOCK = 256
MOBA_TOPK = 3
D_FF = 2816
CONV_WIDTH = 3
ROPE_THETA = 10000.0
NORM_EPS = 1e-6
NEG_INF = -1e30

LANES = 128
LOG2_E = 1.4426950408889634
SCALE = HEAD_DIM ** -0.5 * LOG2_E
MASK_BIAS = -(2.0 ** 100)
VMEM_LIMIT = 56 * 1024 * 1024

TM_PROJ = 512
TQ = 256
FF_CHUNK = 256
N_FF_CHUNKS = D_FF // FF_CHUNK

NSA_WIDTH = NSA_HEADS * HEAD_DIM
MOBA_WIDTH = MOBA_HEADS * HEAD_DIM
KV_WIDTH = NSA_KV_HEADS * HEAD_DIM
IN_SIZES = (NSA_WIDTH, KV_WIDTH, KV_WIDTH, KV_WIDTH, KV_WIDTH, KV_WIDTH, KV_WIDTH,
            3 * NSA_HEADS, MOBA_WIDTH, MOBA_WIDTH, MOBA_WIDTH, D_MODEL, D_MODEL)
IN_OFFSETS = np.concatenate([[0], np.cumsum(IN_SIZES)]).tolist()

NSA_HEAD_ORDER = (0, 4, 1, 5, 2, 6, 3, 7)


def _dot(a, b):
    return jnp.dot(a, b, preferred_element_type=F32)


def _dot_nt(a, b):
    return lax.dot_general(a, b, (((1,), (1,)), ((), ())), preferred_element_type=F32)


def _split_bf16(v):
    hi = v.astype(BF16)
    lo = (v - hi.astype(F32)).astype(BF16)
    return hi, lo


def _cparams(*sem):
    return pltpu.CompilerParams(dimension_semantics=sem, vmem_limit_bytes=VMEM_LIMIT)


def _const_spec(shape):
    n = len(shape)
    return pl.BlockSpec(shape, lambda *_: (0,) * n, pipeline_mode=pl.Buffered(1))


def _adaln_norm(x, g, sc, sh):
    y = x * lax.rsqrt(jnp.mean(x * x, axis=-1, keepdims=True) + NORM_EPS)
    return (y * g) * (1.0 + sc) + sh


def _mod_kernel(c_ref, w_ref, b_ref, o_ref):
    o_ref[...] = jnp.dot(c_ref[...], w_ref[...], preferred_element_type=F32,
                         precision=lax.Precision.HIGHEST) + b_ref[...]


def _mod_call(c, w_ada, b_ada):
    b, d = c.shape
    n = w_ada.shape[1]
    tn = D_MODEL
    return pl.pallas_call(
        _mod_kernel,
        grid=(n // tn,),
        in_specs=[pl.BlockSpec((b, d), lambda j: (0, 0)),
                  pl.BlockSpec((d, tn), lambda j: (0, j)),
                  pl.BlockSpec((1, tn), lambda j: (0, j))],
        out_specs=pl.BlockSpec((b, tn), lambda j: (0, j)),
        out_shape=jax.ShapeDtypeStruct((b, n), F32),
        compiler_params=_cparams("parallel"),
        name="mod",
    )(c, w_ada, b_ada.reshape(1, n))


_P_QN, _P_QM, _P_KM, _P_VM = 0, 512, 1024, 1536
_P_KC, _P_VC, _P_KSL, _P_VSL, _P_KWN, _P_VWN, _P_GN = 2048, 2176, 2304, 2432, 2560, 2688, 2816
_P_WIDTH = 2944


def _inproj_kernel(x_ref, sc_ref, sh_ref, g_ref, pos_ref, invf_ref, w_ref, gains_ref, bd_ref,
                   qn_ref, qm_ref, km_ref, vm_ref, kc_ref, vc_ref, ksl_ref, vsl_ref, kwn_ref,
                   vwn_ref, gn_ref):
    x = x_ref[0]
    tm = x.shape[0]
    h = _adaln_norm(x, g_ref[...], sc_ref[...], sh_ref[...]).astype(BF16)

    ang = pos_ref[0].astype(F32) * invf_ref[...]
    cos = jnp.cos(ang)
    sin = jnp.sin(ang)
    lane = lax.broadcasted_iota(jnp.int32, (tm, LANES), 1)
    first = (lane & (HEAD_DIM // 2)) == 0
    sin_signed = jnp.where(first, -sin, sin)
    bd = bd_ref[...]

    def head_norm(y, gain):
        ms = _dot((y * y).astype(BF16), bd)
        return y * lax.rsqrt(ms + NORM_EPS) * gain

    def rope(y):
        partner = jnp.where(first, pltpu.roll(y, LANES - HEAD_DIM // 2, 1),
                            pltpu.roll(y, HEAD_DIM // 2, 1))
        return y * cos + partner * sin_signed

    def proj(off, width):
        return _dot(h, w_ref[:, off:off + width])

    def wide(off, out_ref, gain_row, scale):
        acc = proj(off, 4 * LANES)
        gain = gains_ref[gain_row:gain_row + 1, :]
        for p in range(4):
            y = rope(head_norm(acc[:, p * LANES:(p + 1) * LANES], gain))
            if scale != 1.0:
                y = y * scale
            out_ref[0, :, p * LANES:(p + 1) * LANES] = y.astype(BF16)

    wide(_P_QN, qn_ref, 0, SCALE)
    wide(_P_QM, qm_ref, 1, SCALE)
    wide(_P_KM, km_ref, 2, 1.0)
    vm_ref[0] = proj(_P_VM, 4 * LANES).astype(BF16)

    small = proj(_P_KC, 7 * LANES)
    kc_ref[0] = rope(small[:, 0:LANES])
    vc_ref[0] = small[:, LANES:2 * LANES]
    ksl_ref[0] = rope(head_norm(small[:, 2 * LANES:3 * LANES], gains_ref[3:4, :])).astype(BF16)
    vsl_ref[0] = small[:, 3 * LANES:4 * LANES].astype(BF16)
    kwn_ref[0] = rope(head_norm(small[:, 4 * LANES:5 * LANES], gains_ref[4:5, :])).astype(BF16)
    vwn_ref[0] = small[:, 5 * LANES:6 * LANES].astype(BF16)
    gn_ref[0] = jax.nn.sigmoid(small[:, 6 * LANES:7 * LANES])


def _inproj_call(x, mod4, g_attn, pos3, inv_freq, w_p, gains, bd):
    b, s, d = x.shape
    tm = min(TM_PROJ, s)
    tok = lambda width: pl.BlockSpec((1, tm, width), lambda bi, i: (bi, i, 0))
    modspec = lambda k: pl.BlockSpec((None, None, 1, d), lambda bi, i: (bi, k, 0, 0))
    shapes = [jax.ShapeDtypeStruct((b, s, 4 * LANES), BF16)] * 4 \
        + [jax.ShapeDtypeStruct((b, s, LANES), F32)] * 2 \
        + [jax.ShapeDtypeStruct((b, s, LANES), BF16)] * 4 \
        + [jax.ShapeDtypeStruct((b, s, LANES), F32)]
    return pl.pallas_call(
        _inproj_kernel,
        grid=(b, s // tm),
        in_specs=[tok(d), modspec(1), modspec(0), _const_spec((1, d)), tok(1),
                  _const_spec((1, LANES)), _const_spec((d, _P_WIDTH)),
                  _const_spec((8, LANES)), _const_spec((LANES, LANES))],
        out_specs=[tok(4 * LANES)] * 4 + [tok(LANES)] * 7,
        out_shape=shapes,
        compiler_params=_cparams("parallel", "parallel"),
        name="inproj",
    )(x, mod4, mod4, g_attn, pos3, inv_freq, w_p, gains, bd)


def _compress_kernel(xk_ref, xv_ref, posk_ref, posv_ref, w1k_ref, w1v_ref, w2k_ref, w2v_ref,
                     gain_ref, bd_ref, ko_ref, vo_ref):
    n_grp = ko_ref.shape[1]

    def mlp(x_ref, pos_ref, w1_ref, w2_ref):
        first = None
        second = None
        for l in range(NSA_CMP_STRIDE):
            xl = x_ref[0, pl.ds(l, n_grp, stride=NSA_CMP_STRIDE), :]
            a = _dot((xl + pos_ref[l:l + 1, :]).astype(BF16), w1_ref[l])
            lb = NSA_CMP_STRIDE + l
            b2 = _dot((xl + pos_ref[lb:lb + 1, :]).astype(BF16), w1_ref[lb])
            first = a if first is None else first + a
            second = b2 if second is None else second + b2
        hidden = first + pltpu.roll(second, n_grp - 1, 0)
        return _dot(jax.nn.gelu(hidden).astype(BF16), w2_ref[...])

    k = mlp(xk_ref, posk_ref, w1k_ref, w2k_ref)
    ms = _dot((k * k).astype(BF16), bd_ref[...])
    ko_ref[0] = (k * lax.rsqrt(ms + NORM_EPS) * gain_ref[...]).astype(BF16)
    vo_ref[0] = mlp(xv_ref, posv_ref, w1v_ref, w2v_ref).astype(BF16)


def _compress_call(xk, xv, posk, posv, w1k, w1v, w2k, w2v, gain, bd):
    b, s, w = xk.shape
    n = s // NSA_CMP_STRIDE
    xspec = pl.BlockSpec((1, s, w), lambda bi: (bi, 0, 0))
    ospec = pl.BlockSpec((1, n, LANES), lambda bi: (bi, 0, 0))
    return pl.pallas_call(
        _compress_kernel,
        grid=(b,),
        in_specs=[xspec, xspec, _const_spec(posk.shape), _const_spec(posv.shape),
                  _const_spec(w1k.shape), _const_spec(w1v.shape), _const_spec(w2k.shape),
                  _const_spec(w2v.shape), _const_spec((1, LANES)), _const_spec((LANES, LANES))],
        out_specs=[ospec, ospec],
        out_shape=[jax.ShapeDtypeStruct((b, n, LANES), BF16)] * 2,
        compiler_params=_cparams("parallel"),
        name="compress",
    )(xk, xv, posk, posv, w1k, w1v, w2k, w2v, gain, bd)


def _lane_half0(tq):
    return lax.broadcasted_iota(jnp.int32, (tq, LANES), 1) < HEAD_DIM


def _dim_half0(tq):
    return lax.broadcasted_iota(jnp.int32, (LANES, tq), 0) < HEAD_DIM


def _take_half(q2, half0, g):
    qf = q2.astype(F32)
    keep = half0 if g == 0 else jnp.logical_not(half0)
    return jnp.where(keep, qf, 0.0).astype(BF16)


def _transposed_bf16(rows):
    return rows.astype(F32).T.astype(BF16)


def _rank_counts(vals, jb, n):
    counts = jnp.zeros_like(vals)
    for j in range(n):
        row = vals[j:j + 1, :]
        beats = jnp.where(vals > row, 1.0,
                          jnp.where(vals == row, jnp.where(jb < j, 1.0, 0.0), 0.0))
        cnt = jnp.sum(beats, axis=0, keepdims=True)
        counts = jnp.where(jb == j, cnt, counts)
    return counts


def _values_t_with_ones(rows, dhalf0):
    vt = rows.astype(F32).T
    return (jnp.where(dhalf0, vt, 1.0).astype(BF16), jnp.where(dhalf0, 1.0, vt).astype(BF16))


def _softmax_max(parts):
    m = None
    for s, _, bias in parts:
        if bias is None:
            mx = jnp.max(s, axis=0, keepdims=True)
        else:
            n_blk = bias.shape[0]
            blk = s.shape[0] // n_blk
            mx = None
            for b in range(n_blk):
                mb = jnp.max(s[b * blk:(b + 1) * blk], axis=0, keepdims=True) + bias[b:b + 1]
                mx = mb if mx is None else jnp.maximum(mx, mb)
        m = mx if m is None else jnp.maximum(m, mx)
    return m


def _softmax_values(parts, m):
    acc = None
    for s, vt, bias in parts:
        if bias is None:
            p = jnp.exp2(s - m)
        else:
            n_blk = bias.shape[0]
            blk = s.shape[0] // n_blk
            p = jnp.concatenate([jnp.exp2(s[b * blk:(b + 1) * blk] - (m - bias[b:b + 1]))
                                 for b in range(n_blk)], axis=0)
        pv = _dot(vt, p.astype(BF16))
        acc = pv if acc is None else acc + pv
    return acc


def _solve_pipelined(problems):
    n = len(problems)
    staged = {}
    for k in range(min(2, n)):
        parts = problems[k](None)
        staged[k] = (parts, _softmax_max(parts))
    out = []
    for k in range(n):
        parts, m = staged.pop(k)
        if k + 2 < n:
            nxt = problems[k + 2]((m[0:1, 0:1] * 0.0).astype(BF16))
            staged[k + 2] = (nxt, _softmax_max(nxt))
        out.append(_softmax_values(parts, m))
    return out


def _anchored(qh, zero):
    return qh if zero is None else qh + zero


def _normalized_pair(acc0, acc1, dhalf0):
    l0 = jnp.maximum(acc0[HEAD_DIM:HEAD_DIM + 1, :], 1e-30)
    l1 = jnp.maximum(acc1[0:1, :], 1e-30)
    return jnp.where(dhalf0, acc0 * (1.0 / l0), acc1 * (1.0 / l1))


def _paired_tile_cases(step, n_tiles, tiles_case):
    for j in range((n_tiles + 1) // 2):
        tiles = [j + 1] if n_tiles - j == j + 1 else [j + 1, n_tiles - j]
        pl.when(step == j)(lambda tiles=tiles: tiles_case(tiles))


def _nsa_cmp_kernel(q_ref, gate_ref, kc_ref, vc_ref, ovt_ref, egt_ref, ocmpt_ref, selt_ref):
    tq = q_ref.shape[1]
    n_c = kc_ref.shape[1]
    t0 = pl.program_id(1) * tq
    g_hi, g_lo = _split_bf16(gate_ref[0])

    c_row = lax.broadcasted_iota(jnp.int32, (n_c, tq), 0)
    t_col = t0 + lax.broadcasted_iota(jnp.int32, (n_c, tq), 1)
    visible = c_row * NSA_CMP_STRIDE + (NSA_CMP_BLOCK - 1) <= t_col
    half0 = _lane_half0(tq)
    dhalf0 = _dim_half0(tq)
    kc = kc_ref[0]
    vct = _transposed_bf16(vc_ref[0])

    psum = [None, None]
    for r in range(NSA_HEADS // NSA_KV_HEADS):
        q2 = q_ref[0, :, r * LANES:(r + 1) * LANES]
        o_pair = None
        for g in range(NSA_KV_HEADS):
            s = _dot_nt(kc, _take_half(q2, half0, g))
            s = jnp.where(visible, s, NEG_INF)
            m = jnp.max(s, axis=0, keepdims=True)
            p = jnp.where(visible, jnp.exp2(s - m), 0.0)
            p = p / jnp.maximum(jnp.sum(p, axis=0, keepdims=True), 1e-30)
            psum[g] = p if psum[g] is None else psum[g] + p
            o = _dot(vct, p.astype(BF16))
            o_pair = o if g == 0 else jnp.where(dhalf0, o_pair, o)
        egt = egt_ref[r * LANES:(r + 1) * LANES, :]
        gate_t = _dot_nt(egt, g_hi) + _dot_nt(egt, g_lo)
        ocmpt_ref[0, r] = gate_t * o_pair

    n_sb = ovt_ref.shape[0]
    jb = lax.broadcasted_iota(jnp.int32, (n_sb, tq), 0)
    own = (t0 + lax.broadcasted_iota(jnp.int32, (n_sb, tq), 1)) // NSA_SEL_BLOCK
    forced = (jb == 0) | (jb == own) | (jb == own - 1)
    ovt = ovt_ref[...]
    for g in range(NSA_KV_HEADS):
        p_hi, p_lo = _split_bf16(psum[g])
        imp = _dot(ovt, p_hi) + _dot(ovt, p_lo)
        imp = jnp.where(jb <= own, imp + jnp.where(forced, NSA_FORCE_BONUS, 0.0), NEG_INF)
        counts = _rank_counts(imp, jb, n_sb)
        selt_ref[0, g] = jnp.where(counts < float(min(NSA_SEL_TOPN, n_sb)), 0.0, MASK_BIAS)


def _nsa_cmp_call(qn, gn, kcmp, vcmp, ovt, eg_cmp_t):
    b, s, w = qn.shape
    tq = min(TQ, s)
    n_pairs = w // LANES
    n_sb = ovt.shape[0]
    tok = lambda width: pl.BlockSpec((1, tq, width), lambda bi, i: (bi, i, 0))
    cspec = pl.BlockSpec((1, kcmp.shape[1], LANES), lambda bi, i: (bi, 0, 0))
    return pl.pallas_call(
        _nsa_cmp_kernel,
        grid=(b, s // tq),
        in_specs=[tok(w), tok(LANES), cspec, cspec, _const_spec(ovt.shape),
                  _const_spec(eg_cmp_t.shape)],
        out_specs=[pl.BlockSpec((1, n_pairs, LANES, tq), lambda bi, i: (bi, 0, 0, i)),
                   pl.BlockSpec((1, NSA_KV_HEADS, n_sb, tq), lambda bi, i: (bi, 0, 0, i))],
        out_shape=[jax.ShapeDtypeStruct((b, n_pairs, LANES, s), F32),
                   jax.ShapeDtypeStruct((b, NSA_KV_HEADS, n_sb, s), F32)],
        compiler_params=_cparams("parallel", "parallel"),
        name="nsa_cmp",
    )(qn, gn, kcmp, vcmp, ovt, eg_cmp_t)


def _nsa_attn_kernel(q_ref, selt_ref, gate_ref, ocmpt_ref, ksl_ref, vsl_ref, kwn_ref, vwn_ref,
                     egst_ref, egwt_ref, o_ref, vslt_ref, vwnt_ref):
    s_len = q_ref.shape[1]
    tq = min(TQ, s_len)
    n_tiles = s_len // tq
    step = pl.program_id(2)
    half0 = _lane_half0(tq)
    dhalf0 = _dim_half0(tq)
    k_row = lax.broadcasted_iota(jnp.int32, (tq, tq), 0)
    q_col = lax.broadcasted_iota(jnp.int32, (tq, tq), 1)
    causal = k_row <= q_col
    blocks_per_chunk = tq // NSA_SEL_BLOCK

    @pl.when(jnp.logical_and(pl.program_id(1) == 0, step == 0))
    def _():
        for c in range(n_tiles):
            cols = slice(c * tq, (c + 1) * tq)
            vslt_ref[0, :, cols], vslt_ref[1, :, cols] = _values_t_with_ones(vsl_ref[0, cols, :],
                                                                             dhalf0)
            vwnt_ref[0, :, cols], vwnt_ref[1, :, cols] = _values_t_with_ones(vwn_ref[0, cols, :],
                                                                             dhalf0)

    def sel_problem(n, qh, g):
        lo = (n - 1) * tq
        all_selected = n * blocks_per_chunk <= NSA_SEL_TOPN
        parts = []
        for c in range(n):
            k0 = c * tq
            sc = _dot_nt(ksl_ref[0, k0:k0 + tq, :], qh)
            if c == n - 1:
                sc = jnp.where(causal, sc, NEG_INF)
            bias = None
            if not all_selected:
                j0 = c * blocks_per_chunk
                bias = selt_ref[0, g, j0:j0 + blocks_per_chunk, lo:lo + tq]
            parts.append((sc, vslt_ref[g, :, k0:k0 + tq], bias))
        return parts

    def win_problem(n, qh, g):
        lo = (n - 1) * tq
        parts = [(jnp.where(causal, _dot_nt(kwn_ref[0, lo:lo + tq, :], qh), NEG_INF),
                  vwnt_ref[g, :, lo:lo + tq], None)]
        for back in range(1, min(n - 1, (NSA_WINDOW + tq - 2) // tq) + 1):
            k0 = lo - back * tq
            sw = _dot_nt(kwn_ref[0, k0:k0 + tq, :], qh)
            if (back + 1) * tq - 1 >= NSA_WINDOW:
                sw = jnp.where(q_col - k_row + back * tq < NSA_WINDOW, sw, NEG_INF)
            parts.append((sw, vwnt_ref[g, :, k0:k0 + tq], None))
        return parts

    def tiles_case(tiles):
        problems = []
        for n in tiles:
            q2 = q_ref[0, (n - 1) * tq:n * tq, :]
            for g in range(NSA_KV_HEADS):
                qh = _take_half(q2, half0, g)
                problems.append(lambda z, n=n, qh=qh, g=g: sel_problem(n, _anchored(qh, z), g))
                problems.append(lambda z, n=n, qh=qh, g=g: win_problem(n, _anchored(qh, z), g))
        accs = _solve_pipelined(problems)
        for i, n in enumerate(tiles):
            lo = (n - 1) * tq
            sel0, win0, sel1, win1 = accs[4 * i:4 * i + 4]
            g_hi, g_lo = _split_bf16(gate_ref[0, lo:lo + tq, :])
            gs_t = _dot_nt(egst_ref[...], g_hi) + _dot_nt(egst_ref[...], g_lo)
            gw_t = _dot_nt(egwt_ref[...], g_hi) + _dot_nt(egwt_ref[...], g_lo)
            out_t = (ocmpt_ref[0, 0, :, lo:lo + tq] + gs_t * _normalized_pair(sel0, sel1, dhalf0)
                     + gw_t * _normalized_pair(win0, win1, dhalf0))
            o_ref[0, lo:lo + tq, :] = out_t.T.astype(BF16)

    _paired_tile_cases(step, n_tiles, tiles_case)


def _nsa_attn_call(qn, selt, gn, ocmpt, ksl, vsl, kwn, vwn, eg_sel_t, eg_win_t):
    b, s, _ = qn.shape
    n_tiles = s // min(TQ, s)
    n_pairs = qn.shape[2] // LANES
    pair = pl.BlockSpec((1, s, LANES), lambda bi, r, j: (bi, 0, r))
    seq = pl.BlockSpec((1, s, LANES), lambda bi, r, j: (bi, 0, 0))
    egspec = pl.BlockSpec((LANES, LANES), lambda bi, r, j: (r, 0))
    return pl.pallas_call(
        _nsa_attn_kernel,
        grid=(b, n_pairs, (n_tiles + 1) // 2),
        in_specs=[pair,
                  pl.BlockSpec((1,) + selt.shape[1:], lambda bi, r, j: (bi, 0, 0, 0)),
                  seq,
                  pl.BlockSpec((1, 1, LANES, s), lambda bi, r, j: (bi, r, 0, 0)),
                  seq, seq, seq, seq, egspec, egspec],
        out_specs=pair,
        out_shape=jax.ShapeDtypeStruct(qn.shape, BF16),
        scratch_shapes=[pltpu.VMEM((NSA_KV_HEADS, LANES, s), BF16)] * 2,
        compiler_params=_cparams("parallel", "arbitrary", "arbitrary"),
        name="nsa_attn",
    )(qn, selt, gn, ocmpt, ksl, vsl, kwn, vwn, eg_sel_t, eg_win_t)


def _moba_kernel(q_ref, k_ref, v_ref, o_ref, vt_ref, kmean_ref):
    s_len = q_ref.shape[1]
    tq = min(MOBA_BLOCK, s_len)
    nb = s_len // tq
    n_top = min(MOBA_TOPK, nb - 1)
    step = pl.program_id(2)
    half0 = _lane_half0(tq)
    dhalf0 = _dim_half0(tq)
    k_row = lax.broadcasted_iota(jnp.int32, (tq, tq), 0)
    q_col = lax.broadcasted_iota(jnp.int32, (tq, tq), 1)
    causal = k_row <= q_col
    jb = lax.broadcasted_iota(jnp.int32, (nb, tq), 0)

    @pl.when(step == 0)
    def _():
        for j in range(nb):
            rows = slice(j * tq, (j + 1) * tq)
            kmean_ref[j:j + 1, :] = jnp.mean(k_ref[0, rows, :].astype(F32), axis=0, keepdims=True)
            vt_ref[0, :, rows], vt_ref[1, :, rows] = _values_t_with_ones(v_ref[0, rows, :], dhalf0)

    def problem(n, qh, h):
        lo = (n - 1) * tq
        n_past = n - 1
        bias = None
        if n_past > n_top:
            km_hi, km_lo = _split_bf16(kmean_ref[...])
            gate = _dot_nt(km_hi, qh) + _dot_nt(km_lo, qh)
            gate = jnp.where(jb < n_past, gate, NEG_INF)
            counts = _rank_counts(gate, jb, n_past)
            bias = jnp.where(counts < float(n_top), 0.0, MASK_BIAS)
        parts = []
        for j in range(n_past):
            sj = _dot_nt(k_ref[0, j * tq:(j + 1) * tq, :], qh)
            parts.append((sj, vt_ref[h, :, j * tq:(j + 1) * tq],
                          None if bias is None else bias[j:j + 1, :]))
        s_own = _dot_nt(k_ref[0, lo:lo + tq, :], qh)
        parts.append((jnp.where(causal, s_own, NEG_INF), vt_ref[h, :, lo:lo + tq], None))
        return parts

    def tiles_case(tiles):
        problems = []
        for n in tiles:
            q2 = q_ref[0, (n - 1) * tq:n * tq, :]
            for h in range(2):
                qh = _take_half(q2, half0, h)
                problems.append(lambda z, n=n, qh=qh, h=h: problem(n, _anchored(qh, z), h))
        accs = _solve_pipelined(problems)
        for i, n in enumerate(tiles):
            out_t = _normalized_pair(accs[2 * i], accs[2 * i + 1], dhalf0)
            o_ref[0, (n - 1) * tq:n * tq, :] = out_t.T.astype(BF16)

    _paired_tile_cases(step, nb, tiles_case)


def _moba_call(qm, km, vm):
    b, s, w = qm.shape
    nb = s // min(MOBA_BLOCK, s)
    n_pairs = w // LANES
    seq = pl.BlockSpec((1, s, LANES), lambda bi, p, j: (bi, 0, p))
    return pl.pallas_call(
        _moba_kernel,
        grid=(b, n_pairs, (nb + 1) // 2),
        in_specs=[seq, seq, seq],
        out_specs=seq,
        out_shape=jax.ShapeDtypeStruct(qm.shape, BF16),
        scratch_shapes=[pltpu.VMEM((2, LANES, s), BF16), pltpu.VMEM((nb, LANES), F32)],
        compiler_params=_cparams("parallel", "parallel", "arbitrary"),
        name="moba",
    )(qm, km, vm)


def _merge_kernel(x_ref, sc_ref, sh_ref, gt_ref, g_ref, oa_ref, ob_ref, wga_ref, wgb_ref,
                  wbn_ref, wbm_ref, wo_ref, o_ref):
    x = x_ref[0]
    h = _adaln_norm(x, g_ref[...], sc_ref[...], sh_ref[...]).astype(BF16)
    ga = jax.nn.sigmoid(_dot(h, wga_ref[...]))
    gb = jax.nn.sigmoid(_dot(h, wgb_ref[...]))
    mixed = ga * _dot(oa_ref[0], wbn_ref[...]) + gb * _dot(ob_ref[0], wbm_ref[...])
    o_ref[0] = x + gt_ref[...] * _dot(mixed.astype(BF16), wo_ref[...])


def _merge_call(x, mod4, g_attn, oa, ob, wga, wgb, wbn, wbm, wo):
    b, s, d = x.shape
    tm = min(TM_PROJ, s)
    tok = lambda width: pl.BlockSpec((1, tm, width), lambda bi, i: (bi, i, 0))
    modspec = lambda k: pl.BlockSpec((None, None, 1, d), lambda bi, i: (bi, k, 0, 0))
    return pl.pallas_call(
        _merge_kernel,
        grid=(b, s // tm),
        in_specs=[tok(d), modspec(1), modspec(0), modspec(2), _const_spec((1, d)),
                  tok(oa.shape[2]), tok(ob.shape[2]), _const_spec(wga.shape),
                  _const_spec(wgb.shape), _const_spec(wbn.shape), _const_spec(wbm.shape),
                  _const_spec(wo.shape)],
        out_specs=tok(d),
        out_shape=jax.ShapeDtypeStruct(x.shape, F32),
        compiler_params=_cparams("parallel", "parallel"),
        name="merge",
    )(x, mod4, mod4, mod4, g_attn, oa, ob, wga, wgb, wbn, wbm, wo)


HALO = 8


def _ffn_kernel(x_ref, halo_ref, sc_ref, sh_ref, gt_ref, g_ref, wa_ref, wv_ref, cw_ref, cb_ref,
                wd_ref, o_ref):
    x = x_ref[0]
    tm = x.shape[0]
    g = g_ref[...]
    h = _adaln_norm(x, g, sc_ref[...], sh_ref[...]).astype(BF16)
    h_halo = _adaln_norm(halo_ref[0], g, sc_ref[...], sh_ref[...]).astype(BF16)
    h_ext = jnp.concatenate([h_halo, h], axis=0)
    ext_row = lax.broadcasted_iota(jnp.int32, (HALO + tm, FF_CHUNK), 0)
    live = ext_row >= jnp.where(pl.program_id(1) > 0, 0, HALO)

    acc = None
    for c in range(wa_ref.shape[0]):
        a = jnp.where(live, _dot(h_ext, wa_ref[c]), 0.0)
        cw = cw_ref[c]
        y = cb_ref[c] + cw[CONV_WIDTH - 1:CONV_WIDTH, :] * a[HALO:, :]
        for back in range(1, CONV_WIDTH):
            k = CONV_WIDTH - 1 - back
            y = y + cw[k:k + 1, :] * pltpu.roll(a, back, 0)[HALO:, :]
        gated = jax.nn.gelu(y) * _dot(h, wv_ref[c])
        part = _dot(gated.astype(BF16), wd_ref[c])
        acc = part if acc is None else acc + part
    o_ref[0] = x + gt_ref[...] * acc


def _ffn_call(x, mod4, g_ffn, wa, wv, cw, cb, wd):
    b, s, d = x.shape
    tm = min(TM_PROJ, s)
    tok = pl.BlockSpec((1, tm, d), lambda bi, i: (bi, i, 0))
    halo = pl.BlockSpec((1, HALO, d), lambda bi, i: (bi, jnp.maximum(i * (tm // HALO) - 1, 0), 0))
    modspec = lambda k: pl.BlockSpec((None, None, 1, d), lambda bi, i: (bi, k, 0, 0))
    return pl.pallas_call(
        _ffn_kernel,
        grid=(b, s // tm),
        in_specs=[tok, halo, modspec(4), modspec(3), modspec(5), _const_spec((1, d)),
                  _const_spec(wa.shape), _const_spec(wv.shape), _const_spec(cw.shape),
                  _const_spec(cb.shape), _const_spec(wd.shape)],
        out_specs=tok,
        out_shape=jax.ShapeDtypeStruct(x.shape, F32),
        compiler_params=_cparams("parallel", "parallel"),
        name="ffn",
    )(x, x, mod4, mod4, mod4, g_ffn, wa, wv, cw, cb, wd)


def _block_diag_mean():
    lane = np.arange(LANES)
    return (lane[:, None] // HEAD_DIM == lane[None, :] // HEAD_DIM).astype(np.float32) / HEAD_DIM


def _overlap_t(n_cmp, n_cmp_pad, n_sb):
    c_start = np.arange(n_cmp)[:, None] * NSA_CMP_STRIDE
    js = np.arange(n_sb)[None, :]
    ov = ((c_start < (js + 1) * NSA_SEL_BLOCK) & (c_start + NSA_CMP_BLOCK > js * NSA_SEL_BLOCK))
    out = np.zeros((n_sb, n_cmp_pad), np.float32)
    out[:, :n_cmp] = ov.T
    return out


def _gate_expand_t(branch):
    out = np.zeros((NSA_WIDTH, LANES), np.float32)
    for slot, head in enumerate(NSA_HEAD_ORDER):
        out[slot * HEAD_DIM:(slot + 1) * HEAD_DIM, 3 * head + branch] = 1.0
    return out


def _block_diag2(w):
    z = jnp.zeros_like(w)
    return jnp.concatenate([jnp.concatenate([w, z], axis=1), jnp.concatenate([z, w], axis=1)], axis=0)


def _layer(x, c, positions, w_ada, b_ada, g_attn_norm, w_in, g_q_nsa, g_k_cmp, g_k_slc, g_k_win,
           cmp_k_pos, cmp_k_w1, cmp_k_w2, cmp_v_pos, cmp_v_w1, cmp_v_w2, g_q_moba, g_k_moba,
           w_branch_nsa, w_branch_moba, w_out, g_ffn_norm, w_ffn_up, conv_w, conv_b, w_ffn_down):
    b, s, d = x.shape
    order = np.asarray(NSA_HEAD_ORDER)
    off = IN_OFFSETS

    mod = _mod_call(c, w_ada, b_ada)
    mod4 = mod.reshape(b, 6, 1, d)

    col = lambda k: w_in[:, off[k]:off[k + 1]]
    w_qn = col(0).reshape(d, NSA_HEADS, HEAD_DIM)[:, order].reshape(d, NSA_WIDTH)
    w_gn = jnp.pad(col(7), ((0, 0), (0, LANES - 3 * NSA_HEADS)))
    w_p = jnp.concatenate([w_qn, col(8), col(9), col(10), col(1), col(2), col(3), col(4), col(5),
                           col(6), w_gn], axis=1).astype(BF16)
    tile2 = lambda gv: jnp.tile(gv, LANES // HEAD_DIM)
    gains = jnp.stack([tile2(g_q_nsa), tile2(g_q_moba), tile2(g_k_moba), tile2(g_k_slc),
                       tile2(g_k_win)] + [jnp.ones((LANES,), F32)] * 3)
    bd = jnp.asarray(_block_diag_mean(), BF16)
    half = HEAD_DIM // 2
    inv_freq = ROPE_THETA ** (-jnp.arange(half, dtype=F32) / half)
    inv_freq = jnp.tile(inv_freq, LANES // half).reshape(1, LANES)

    (qn, qm, km, vm, kc, vc, ksl, vsl, kwn, vwn, gn) = _inproj_call(
        x, mod4, g_attn_norm.reshape(1, d), positions.reshape(b, s, 1), inv_freq, w_p, gains, bd)

    n_grp = s // NSA_CMP_STRIDE
    n_cmp = (s - NSA_CMP_BLOCK) // NSA_CMP_STRIDE + 1
    w1_pack = lambda w1: jax.vmap(_block_diag2)(
        w1.reshape(NSA_CMP_BLOCK, HEAD_DIM, HEAD_DIM)).astype(BF16)
    pos_pack = lambda p: jnp.tile(p, (1, NSA_KV_HEADS))
    kcmp, vcmp = _compress_call(
        kc, vc, pos_pack(cmp_k_pos), pos_pack(cmp_v_pos),
        w1_pack(cmp_k_w1), w1_pack(cmp_v_w1), _block_diag2(cmp_k_w2).astype(BF16),
        _block_diag2(cmp_v_w2).astype(BF16), tile2(g_k_cmp).reshape(1, LANES), bd)

    n_sb = s // NSA_SEL_BLOCK
    ovt = jnp.asarray(_overlap_t(n_cmp, n_grp, n_sb), BF16)
    ocmpt, selt = _nsa_cmp_call(qn, gn, kcmp, vcmp, ovt, jnp.asarray(_gate_expand_t(0), BF16))
    oa = _nsa_attn_call(qn, selt, gn, ocmpt, ksl, vsl, kwn, vwn,
                        jnp.asarray(_gate_expand_t(1), BF16), jnp.asarray(_gate_expand_t(2), BF16))

    ob = _moba_call(qm, km, vm)

    w_bn = w_branch_nsa.reshape(NSA_HEADS, HEAD_DIM, d)[order].reshape(NSA_WIDTH, d)
    x1 = _merge_call(x, mod4, g_attn_norm.reshape(1, d), oa, ob, col(11).astype(BF16),
                     col(12).astype(BF16), w_bn.astype(BF16), w_branch_moba.astype(BF16),
                     w_out.astype(BF16))

    chunks = lambda w: w.reshape(w.shape[0], N_FF_CHUNKS, FF_CHUNK).transpose(1, 0, 2)
    wa = chunks(w_ffn_up[:, :D_FF]).astype(BF16)
    wv = chunks(w_ffn_up[:, D_FF:]).astype(BF16)
    cw = chunks(conv_w)
    cb = chunks(conv_b.reshape(1, D_FF))
    wd = w_ffn_down.reshape(N_FF_CHUNKS, FF_CHUNK, d).astype(BF16)
    return _ffn_call(x1, mod4, g_ffn_norm.reshape(1, d), wa, wv, cw, cb, wd)


def kernel(x, c, positions, w_ada, b_ada, g_attn_norm, w_in, g_q_nsa, g_k_cmp, g_k_slc, g_k_win,
           cmp_k_pos, cmp_k_w1, cmp_k_w2, cmp_v_pos, cmp_v_w1, cmp_v_w2, g_q_moba, g_k_moba,
           w_branch_nsa, w_branch_moba, w_out, g_ffn_norm, w_ffn_up, conv_w, conv_b, w_ffn_down):
    for l in range(w_ada.shape[0]):
        x = _layer(x, c, positions, w_ada[l], b_ada[l], g_attn_norm[l], w_in[l], g_q_nsa[l],
                   g_k_cmp[l], g_k_slc[l], g_k_win[l], cmp_k_pos[l], cmp_k_w1[l], cmp_k_w2[l],
                   cmp_v_pos[l], cmp_v_w1[l], cmp_v_w2[l], g_q_moba[l], g_k_moba[l],
                   w_branch_nsa[l], w_branch_moba[l], w_out[l], g_ffn_norm[l], w_ffn_up[l],
                   conv_w[l], conv_b[l], w_ffn_down[l])
    return x
```

```python
import numpy as np
import jax
import jax.numpy as jnp
from jax import lax
from jax.experimental import pallas as pl
from jax.experimental.pallas import tpu as pltpu

F32 = jnp.float32
BF16 = jnp.bfloat16

D_MODEL = 1024
HEAD_DIM = 64
NSA_HEADS = 8
NSA_KV_HEADS = 2
NSA_CMP_BLOCK = 32
NSA_CMP_STRIDE = 16
NSA_SEL_BLOCK = 64
NSA_SEL_TOPN = 16
NSA_WINDOW = 512
NSA_FORCE_BONUS = 1e4
MOBA_HEADS = 8
MOBA_BLOCK = 256
MOBA_TOPK = 3
D_FF = 2816
CONV_WIDTH = 3
ROPE_THETA = 10000.0
NORM_EPS = 1e-6
NEG_INF = -1e30

LANES = 128
LOG2_E = 1.4426950408889634
SCALE = HEAD_DIM ** -0.5 * LOG2_E
MASK_BIAS = -(2.0 ** 100)
VMEM_LIMIT = 56 * 1024 * 1024

TM_PROJ = 512
TQ = 256
FF_CHUNK = 256
N_FF_CHUNKS = D_FF // FF_CHUNK

NSA_WIDTH = NSA_HEADS * HEAD_DIM
MOBA_WIDTH = MOBA_HEADS * HEAD_DIM
KV_WIDTH = NSA_KV_HEADS * HEAD_DIM
IN_SIZES = (NSA_WIDTH, KV_WIDTH, KV_WIDTH, KV_WIDTH, KV_WIDTH, KV_WIDTH, KV_WIDTH,
            3 * NSA_HEADS, MOBA_WIDTH, MOBA_WIDTH, MOBA_WIDTH, D_MODEL, D_MODEL)
IN_OFFSETS = np.concatenate([[0], np.cumsum(IN_SIZES)]).tolist()

NSA_HEAD_ORDER = (0, 4, 1, 5, 2, 6, 3, 7)


def _dot(a, b):
    return jnp.dot(a, b, preferred_element_type=F32)


def _dot_nt(a, b):
    return lax.dot_general(a, b, (((1,), (1,)), ((), ())), preferred_element_type=F32)


def _split_bf16(v):
    hi = v.astype(BF16)
    lo = (v - hi.astype(F32)).astype(BF16)
    return hi, lo


def _cparams(*sem, flags=None):
    return pltpu.CompilerParams(dimension_semantics=sem, vmem_limit_bytes=VMEM_LIMIT, flags=flags)


def _const_spec(shape):
    n = len(shape)
    return pl.BlockSpec(shape, lambda *_: (0,) * n, pipeline_mode=pl.Buffered(1))


def _adaln_norm(x, g, sc, sh):
    y = x * lax.rsqrt(jnp.mean(x * x, axis=-1, keepdims=True) + NORM_EPS)
    return (y * g) * (1.0 + sc) + sh


def _mod_kernel(c_ref, w_ref, b_ref, o_ref):
    o_ref[...] = jnp.dot(c_ref[...], w_ref[...], preferred_element_type=F32,
                         precision=lax.Precision.HIGHEST) + b_ref[...]


def _mod_call(c, w_ada, b_ada):
    b, d = c.shape
    n = w_ada.shape[1]
    tn = D_MODEL
    return pl.pallas_call(
        _mod_kernel,
        grid=(n // tn,),
        in_specs=[pl.BlockSpec((b, d), lambda j: (0, 0)),
                  pl.BlockSpec((d, tn), lambda j: (0, j)),
                  pl.BlockSpec((1, tn), lambda j: (0, j))],
        out_specs=pl.BlockSpec((b, tn), lambda j: (0, j)),
        out_shape=jax.ShapeDtypeStruct((b, n), F32),
        compiler_params=_cparams("parallel"),
        name="mod",
    )(c, w_ada, b_ada.reshape(1, n))


_P_QN, _P_QM, _P_KM, _P_VM = 0, 512, 1024, 1536
_P_KC, _P_VC, _P_KSL, _P_VSL, _P_KWN, _P_VWN, _P_GN = 2048, 2176, 2304, 2432, 2560, 2688, 2816
_P_WIDTH = 2944


def _inproj_kernel(x_ref, sc_ref, sh_ref, g_ref, pos_ref, invf_ref, w_ref, gains_ref, bd_ref,
                   qn_ref, qm_ref, km_ref, vm_ref, kc_ref, vc_ref, ksl_ref, vsl_ref, kwn_ref,
                   vwn_ref, gn_ref):
    x = x_ref[0]
    tm = x.shape[0]
    h = _adaln_norm(x, g_ref[...], sc_ref[...], sh_ref[...]).astype(BF16)

    ang = pos_ref[0].astype(F32) * invf_ref[...]
    cos = jnp.cos(ang)
    sin = jnp.sin(ang)
    lane = lax.broadcasted_iota(jnp.int32, (tm, LANES), 1)
    first = (lane & (HEAD_DIM // 2)) == 0
    sin_signed = jnp.where(first, -sin, sin)
    bd = bd_ref[...]

    def head_norm(y, gain):
        ms = _dot((y * y).astype(BF16), bd)
        return y * lax.rsqrt(ms + NORM_EPS) * gain

    def rope(y):
        partner = jnp.where(first, pltpu.roll(y, LANES - HEAD_DIM // 2, 1),
                            pltpu.roll(y, HEAD_DIM // 2, 1))
        return y * cos + partner * sin_signed

    def proj(off, width):
        return _dot(h, w_ref[:, off:off + width])

    def wide(off, out_ref, gain_row, scale):
        acc = proj(off, 4 * LANES)
        gain = gains_ref[gain_row:gain_row + 1, :]
        for p in range(4):
            y = rope(head_norm(acc[:, p * LANES:(p + 1) * LANES], gain))
            if scale != 1.0:
                y = y * scale
            out_ref[0, :, p * LANES:(p + 1) * LANES] = y.astype(BF16)

    wide(_P_QN, qn_ref, 0, SCALE)
    wide(_P_QM, qm_ref, 1, SCALE)
    wide(_P_KM, km_ref, 2, 1.0)
    vm_ref[0] = proj(_P_VM, 4 * LANES).astype(BF16)

    small = proj(_P_KC, 7 * LANES)
    kc_ref[0] = rope(small[:, 0:LANES])
    vc_ref[0] = small[:, LANES:2 * LANES]
    ksl_ref[0] = rope(head_norm(small[:, 2 * LANES:3 * LANES], gains_ref[3:4, :])).astype(BF16)
    vsl_ref[0] = small[:, 3 * LANES:4 * LANES].astype(BF16)
    kwn_ref[0] = rope(head_norm(small[:, 4 * LANES:5 * LANES], gains_ref[4:5, :])).astype(BF16)
    vwn_ref[0] = small[:, 5 * LANES:6 * LANES].astype(BF16)
    gn_ref[0] = jax.nn.sigmoid(small[:, 6 * LANES:7 * LANES])


def _inproj_call(x, mod4, g_attn, pos3, inv_freq, w_p, gains, bd):
    b, s, d = x.shape
    tm = min(TM_PROJ, s)
    tok = lambda width: pl.BlockSpec((1, tm, width), lambda bi, i: (bi, i, 0))
    modspec = lambda k: pl.BlockSpec((None, None, 1, d), lambda bi, i: (bi, k, 0, 0))
    shapes = [jax.ShapeDtypeStruct((b, s, 4 * LANES), BF16)] * 4 \
        + [jax.ShapeDtypeStruct((b, s, LANES), F32)] * 2 \
        + [jax.ShapeDtypeStruct((b, s, LANES), BF16)] * 4 \
        + [jax.ShapeDtypeStruct((b, s, LANES), F32)]
    return pl.pallas_call(
        _inproj_kernel,
        grid=(b, s // tm),
        in_specs=[tok(d), modspec(1), modspec(0), _const_spec((1, d)), tok(1),
                  _const_spec((1, LANES)), _const_spec((d, _P_WIDTH)),
                  _const_spec((8, LANES)), _const_spec((LANES, LANES))],
        out_specs=[tok(4 * LANES)] * 4 + [tok(LANES)] * 7,
        out_shape=shapes,
        compiler_params=_cparams("parallel", "parallel"),
        name="inproj",
    )(x, mod4, mod4, g_attn, pos3, inv_freq, w_p, gains, bd)


def _compress_kernel(xk_ref, xv_ref, posk_ref, posv_ref, w1k_ref, w1v_ref, w2k_ref, w2v_ref,
                     gain_ref, bd_ref, ko_ref, vo_ref):
    n_grp = ko_ref.shape[1]

    def mlp(x_ref, pos_ref, w1_ref, w2_ref):
        first = None
        second = None
        for l in range(NSA_CMP_STRIDE):
            xl = x_ref[0, pl.ds(l, n_grp, stride=NSA_CMP_STRIDE), :]
            a = _dot((xl + pos_ref[l:l + 1, :]).astype(BF16), w1_ref[l])
            lb = NSA_CMP_STRIDE + l
            b2 = _dot((xl + pos_ref[lb:lb + 1, :]).astype(BF16), w1_ref[lb])
            first = a if first is None else first + a
            second = b2 if second is None else second + b2
        hidden = first + pltpu.roll(second, n_grp - 1, 0)
        return _dot(jax.nn.gelu(hidden).astype(BF16), w2_ref[...])

    k = mlp(xk_ref, posk_ref, w1k_ref, w2k_ref)
    ms = _dot((k * k).astype(BF16), bd_ref[...])
    ko_ref[0] = (k * lax.rsqrt(ms + NORM_EPS) * gain_ref[...]).astype(BF16)
    vo_ref[0] = mlp(xv_ref, posv_ref, w1v_ref, w2v_ref).astype(BF16)


def _compress_call(xk, xv, posk, posv, w1k, w1v, w2k, w2v, gain, bd):
    b, s, w = xk.shape
    n = s // NSA_CMP_STRIDE
    xspec = pl.BlockSpec((1, s, w), lambda bi: (bi, 0, 0))
    ospec = pl.BlockSpec((1, n, LANES), lambda bi: (bi, 0, 0))
    return pl.pallas_call(
        _compress_kernel,
        grid=(b,),
        in_specs=[xspec, xspec, _const_spec(posk.shape), _const_spec(posv.shape),
                  _const_spec(w1k.shape), _const_spec(w1v.shape), _const_spec(w2k.shape),
                  _const_spec(w2v.shape), _const_spec((1, LANES)), _const_spec((LANES, LANES))],
        out_specs=[ospec, ospec],
        out_shape=[jax.ShapeDtypeStruct((b, n, LANES), BF16)] * 2,
        compiler_params=_cparams("parallel"),
        name="compress",
    )(xk, xv, posk, posv, w1k, w1v, w2k, w2v, gain, bd)


def _lane_half0(tq):
    return lax.broadcasted_iota(jnp.int32, (tq, LANES), 1) < HEAD_DIM


def _dim_half0(tq):
    return lax.broadcasted_iota(jnp.int32, (LANES, tq), 0) < HEAD_DIM


def _take_half(q2, half0, g):
    qf = q2.astype(F32)
    keep = half0 if g == 0 else jnp.logical_not(half0)
    return jnp.where(keep, qf, 0.0).astype(BF16)


def _transposed_bf16(rows):
    return rows.astype(F32).T.astype(BF16)


def _rank_counts(vals, jb, n):
    counts = jnp.zeros_like(vals)
    for j in range(n):
        row = vals[j:j + 1, :]
        beats = jnp.where(vals > row, 1.0,
                          jnp.where(vals == row, jnp.where(jb < j, 1.0, 0.0), 0.0))
        cnt = jnp.sum(beats, axis=0, keepdims=True)
        counts = jnp.where(jb == j, cnt, counts)
    return counts


def _values_t_with_ones(rows, dhalf0):
    vt = rows.astype(F32).T
    return (jnp.where(dhalf0, vt, 1.0).astype(BF16), jnp.where(dhalf0, 1.0, vt).astype(BF16))


def _softmax_max(parts):
    m = None
    for s, _, bias in parts:
        if bias is None:
            mx = jnp.max(s, axis=0, keepdims=True)
        else:
            n_blk = bias.shape[0]
            blk = s.shape[0] // n_blk
            mx = None
            for b in range(n_blk):
                mb = jnp.max(s[b * blk:(b + 1) * blk], axis=0, keepdims=True) + bias[b:b + 1]
                mx = mb if mx is None else jnp.maximum(mx, mb)
        m = mx if m is None else jnp.maximum(m, mx)
    return m


def _softmax_values(parts, m):
    acc = None
    for s, vt, bias in parts:
        if bias is None:
            p = jnp.exp2(s - m)
        else:
            n_blk = bias.shape[0]
            blk = s.shape[0] // n_blk
            p = jnp.concatenate([jnp.exp2(s[b * blk:(b + 1) * blk] - (m - bias[b:b + 1]))
                                 for b in range(n_blk)], axis=0)
        pv = _dot(vt, p.astype(BF16))
        acc = pv if acc is None else acc + pv
    return acc


def _solve_all(problems):
    all_parts = [problem() for problem in problems]
    maxima = [_softmax_max(parts) for parts in all_parts]
    return [_softmax_values(parts, m) for parts, m in zip(all_parts, maxima)]


def _normalized_pair(acc0, acc1, dhalf0):
    l0 = jnp.maximum(acc0[HEAD_DIM:HEAD_DIM + 1, :], 1e-30)
    l1 = jnp.maximum(acc1[0:1, :], 1e-30)
    return jnp.where(dhalf0, acc0 * (1.0 / l0), acc1 * (1.0 / l1))


NSA_TILES_PER_STEP = 4
MOBA_TILES_PER_STEP = 8


def _tile_groups(n_tiles, tiles_per_step):
    pairs = [[j + 1] if n_tiles - j == j + 1 else [j + 1, n_tiles - j]
             for j in range((n_tiles + 1) // 2)]
    per_group = max(tiles_per_step // 2, 1)
    return [sum(pairs[i:i + per_group], []) for i in range(0, len(pairs), per_group)]


def _grouped_tile_cases(step, groups, tiles_case):
    for j, tiles in enumerate(groups):
        pl.when(step == j)(lambda tiles=tiles: tiles_case(tiles))


def _nsa_cmp_kernel(q_ref, gate_ref, kc_ref, vc_ref, ovt_ref, egt_ref, ocmpt_ref, selt_ref):
    tq = q_ref.shape[1]
    n_c = kc_ref.shape[1]
    t0 = pl.program_id(1) * tq
    g_hi, g_lo = _split_bf16(gate_ref[0])

    c_row = lax.broadcasted_iota(jnp.int32, (n_c, tq), 0)
    t_col = t0 + lax.broadcasted_iota(jnp.int32, (n_c, tq), 1)
    visible = c_row * NSA_CMP_STRIDE + (NSA_CMP_BLOCK - 1) <= t_col
    half0 = _lane_half0(tq)
    dhalf0 = _dim_half0(tq)
    kc = kc_ref[0]
    vct = _transposed_bf16(vc_ref[0])

    n_pairs = NSA_HEADS // NSA_KV_HEADS
    scores = [[_dot_nt(kc, _take_half(q_ref[0, :, r * LANES:(r + 1) * LANES], half0, g))
               for g in range(NSA_KV_HEADS)] for r in range(n_pairs)]
    psum = [None, None]
    for r in range(n_pairs):
        o_pair = None
        for g in range(NSA_KV_HEADS):
            s = jnp.where(visible, scores[r][g], NEG_INF)
            m = jnp.max(s, axis=0, keepdims=True)
            p = jnp.where(visible, jnp.exp2(s - m), 0.0)
            p = p / jnp.maximum(jnp.sum(p, axis=0, keepdims=True), 1e-30)
            psum[g] = p if psum[g] is None else psum[g] + p
            o = _dot(vct, p.astype(BF16))
            o_pair = o if g == 0 else jnp.where(dhalf0, o_pair, o)
        egt = egt_ref[r * LANES:(r + 1) * LANES, :]
        gate_t = _dot_nt(egt, g_hi) + _dot_nt(egt, g_lo)
        ocmpt_ref[0, r] = gate_t * o_pair

    n_sb = ovt_ref.shape[0]
    jb = lax.broadcasted_iota(jnp.int32, (n_sb, tq), 0)
    own = (t0 + lax.broadcasted_iota(jnp.int32, (n_sb, tq), 1)) // NSA_SEL_BLOCK
    forced = (jb == 0) | (jb == own) | (jb == own - 1)
    ovt = ovt_ref[...]
    for g in range(NSA_KV_HEADS):
        p_hi, p_lo = _split_bf16(psum[g])
        imp = _dot(ovt, p_hi) + _dot(ovt, p_lo)
        imp = jnp.where(jb <= own, imp + jnp.where(forced, NSA_FORCE_BONUS, 0.0), NEG_INF)
        counts = _rank_counts(imp, jb, n_sb)
        selt_ref[0, g] = jnp.where(counts < float(min(NSA_SEL_TOPN, n_sb)), 0.0, MASK_BIAS)


def _nsa_cmp_call(qn, gn, kcmp, vcmp, ovt, eg_cmp_t):
    b, s, w = qn.shape
    tq = min(TQ, s)
    n_pairs = w // LANES
    n_sb = ovt.shape[0]
    tok = lambda width: pl.BlockSpec((1, tq, width), lambda bi, i: (bi, i, 0))
    cspec = pl.BlockSpec((1, kcmp.shape[1], LANES), lambda bi, i: (bi, 0, 0))
    return pl.pallas_call(
        _nsa_cmp_kernel,
        grid=(b, s // tq),
        in_specs=[tok(w), tok(LANES), cspec, cspec, _const_spec(ovt.shape),
                  _const_spec(eg_cmp_t.shape)],
        out_specs=[pl.BlockSpec((1, n_pairs, LANES, tq), lambda bi, i: (bi, 0, 0, i)),
                   pl.BlockSpec((1, NSA_KV_HEADS, n_sb, tq), lambda bi, i: (bi, 0, 0, i))],
        out_shape=[jax.ShapeDtypeStruct((b, n_pairs, LANES, s), F32),
                   jax.ShapeDtypeStruct((b, NSA_KV_HEADS, n_sb, s), F32)],
        compiler_params=_cparams("parallel", "parallel"),
        name="nsa_cmp",
    )(qn, gn, kcmp, vcmp, ovt, eg_cmp_t)


def _nsa_attn_kernel(q_ref, selt_ref, gate_ref, ocmpt_ref, ksl_ref, vsl_ref, kwn_ref, vwn_ref,
                     egst_ref, egwt_ref, o_ref, vslt_ref, vwnt_ref):
    s_len = q_ref.shape[1]
    tq = min(TQ, s_len)
    n_tiles = s_len // tq
    step = pl.program_id(2)
    half0 = _lane_half0(tq)
    dhalf0 = _dim_half0(tq)
    k_row = lax.broadcasted_iota(jnp.int32, (tq, tq), 0)
    q_col = lax.broadcasted_iota(jnp.int32, (tq, tq), 1)
    causal = k_row <= q_col
    blocks_per_chunk = tq // NSA_SEL_BLOCK

    @pl.when(jnp.logical_and(pl.program_id(1) == 0, step == 0))
    def _():
        for c in range(n_tiles):
            cols = slice(c * tq, (c + 1) * tq)
            vslt_ref[0, :, cols], vslt_ref[1, :, cols] = _values_t_with_ones(vsl_ref[0, cols, :],
                                                                             dhalf0)
            vwnt_ref[0, :, cols], vwnt_ref[1, :, cols] = _values_t_with_ones(vwn_ref[0, cols, :],
                                                                             dhalf0)

    def sel_problem(n, qh, g):
        lo = (n - 1) * tq
        all_selected = n * blocks_per_chunk <= NSA_SEL_TOPN
        parts = []
        for c in range(n):
            k0 = c * tq
            sc = _dot_nt(ksl_ref[0, k0:k0 + tq, :], qh)
            if c == n - 1:
                sc = jnp.where(causal, sc, NEG_INF)
            bias = None
            if not all_selected:
                j0 = c * blocks_per_chunk
                bias = selt_ref[0, g, j0:j0 + blocks_per_chunk, lo:lo + tq]
            parts.append((sc, vslt_ref[g, :, k0:k0 + tq], bias))
        return parts

    def win_problem(n, qh, g):
        lo = (n - 1) * tq
        parts = [(jnp.where(causal, _dot_nt(kwn_ref[0, lo:lo + tq, :], qh), NEG_INF),
                  vwnt_ref[g, :, lo:lo + tq], None)]
        for back in range(1, min(n - 1, (NSA_WINDOW + tq - 2) // tq) + 1):
            k0 = lo - back * tq
            sw = _dot_nt(kwn_ref[0, k0:k0 + tq, :], qh)
            if (back + 1) * tq - 1 >= NSA_WINDOW:
                sw = jnp.where(q_col - k_row + back * tq < NSA_WINDOW, sw, NEG_INF)
            parts.append((sw, vwnt_ref[g, :, k0:k0 + tq], None))
        return parts

    def tiles_case(tiles):
        problems = []
        for n in tiles:
            q2 = q_ref[0, (n - 1) * tq:n * tq, :]
            for g in range(NSA_KV_HEADS):
                qh = _take_half(q2, half0, g)
                problems.append(lambda n=n, qh=qh, g=g: sel_problem(n, qh, g))
                problems.append(lambda n=n, qh=qh, g=g: win_problem(n, qh, g))
        accs = _solve_all(problems)
        for i, n in enumerate(tiles):
            lo = (n - 1) * tq
            sel0, win0, sel1, win1 = accs[4 * i:4 * i + 4]
            g_hi, g_lo = _split_bf16(gate_ref[0, lo:lo + tq, :])
            gs_t = _dot_nt(egst_ref[...], g_hi) + _dot_nt(egst_ref[...], g_lo)
            gw_t = _dot_nt(egwt_ref[...], g_hi) + _dot_nt(egwt_ref[...], g_lo)
            out_t = (ocmpt_ref[0, 0, :, lo:lo + tq] + gs_t * _normalized_pair(sel0, sel1, dhalf0)
                     + gw_t * _normalized_pair(win0, win1, dhalf0))
            o_ref[0, lo:lo + tq, :] = out_t.T.astype(BF16)

    _grouped_tile_cases(step, _tile_groups(n_tiles, NSA_TILES_PER_STEP), tiles_case)


def _nsa_attn_call(qn, selt, gn, ocmpt, ksl, vsl, kwn, vwn, eg_sel_t, eg_win_t):
    b, s, _ = qn.shape
    n_tiles = s // min(TQ, s)
    n_pairs = qn.shape[2] // LANES
    pair = pl.BlockSpec((1, s, LANES), lambda bi, r, j: (bi, 0, r))
    seq = pl.BlockSpec((1, s, LANES), lambda bi, r, j: (bi, 0, 0))
    egspec = pl.BlockSpec((LANES, LANES), lambda bi, r, j: (r, 0))
    return pl.pallas_call(
        _nsa_attn_kernel,
        grid=(b, n_pairs, len(_tile_groups(n_tiles, NSA_TILES_PER_STEP))),
        in_specs=[pair,
                  pl.BlockSpec((1,) + selt.shape[1:], lambda bi, r, j: (bi, 0, 0, 0)),
                  seq,
                  pl.BlockSpec((1, 1, LANES, s), lambda bi, r, j: (bi, r, 0, 0)),
                  seq, seq, seq, seq, egspec, egspec],
        out_specs=pair,
        out_shape=jax.ShapeDtypeStruct(qn.shape, BF16),
        scratch_shapes=[pltpu.VMEM((NSA_KV_HEADS, LANES, s), BF16)] * 2,
        compiler_params=_cparams("parallel", "arbitrary", "arbitrary"),
        name="nsa_attn",
    )(qn, selt, gn, ocmpt, ksl, vsl, kwn, vwn, eg_sel_t, eg_win_t)


def _moba_kernel(q_ref, k_ref, v_ref, o_ref, vt_ref, kmean_ref):
    s_len = q_ref.shape[1]
    tq = min(MOBA_BLOCK, s_len)
    nb = s_len // tq
    n_top = min(MOBA_TOPK, nb - 1)
    step = pl.program_id(2)
    half0 = _lane_half0(tq)
    dhalf0 = _dim_half0(tq)
    k_row = lax.broadcasted_iota(jnp.int32, (tq, tq), 0)
    q_col = lax.broadcasted_iota(jnp.int32, (tq, tq), 1)
    causal = k_row <= q_col
    jb = lax.broadcasted_iota(jnp.int32, (nb, tq), 0)

    @pl.when(step == 0)
    def _():
        for j in range(nb):
            rows = slice(j * tq, (j + 1) * tq)
            kmean_ref[j:j + 1, :] = jnp.mean(k_ref[0, rows, :].astype(F32), axis=0, keepdims=True)
            vt_ref[0, :, rows], vt_ref[1, :, rows] = _values_t_with_ones(v_ref[0, rows, :], dhalf0)

    def problem(n, qh, h):
        lo = (n - 1) * tq
        n_past = n - 1
        bias = None
        if n_past > n_top:
            km_hi, km_lo = _split_bf16(kmean_ref[...])
            gate = _dot_nt(km_hi, qh) + _dot_nt(km_lo, qh)
            gate = jnp.where(jb < n_past, gate, NEG_INF)
            counts = _rank_counts(gate, jb, n_past)
            bias = jnp.where(counts < float(n_top), 0.0, MASK_BIAS)
        parts = []
        for j in range(n_past):
            sj = _dot_nt(k_ref[0, j * tq:(j + 1) * tq, :], qh)
            parts.append((sj, vt_ref[h, :, j * tq:(j + 1) * tq],
                          None if bias is None else bias[j:j + 1, :]))
        s_own = _dot_nt(k_ref[0, lo:lo + tq, :], qh)
        parts.append((jnp.where(causal, s_own, NEG_INF), vt_ref[h, :, lo:lo + tq], None))
        return parts

    def tiles_case(tiles):
        problems = []
        for n in tiles:
            q2 = q_ref[0, (n - 1) * tq:n * tq, :]
            for h in range(2):
                qh = _take_half(q2, half0, h)
                problems.append(lambda n=n, qh=qh, h=h: problem(n, qh, h))
        accs = _solve_all(problems)
        for i, n in enumerate(tiles):
            out_t = _normalized_pair(accs[2 * i], accs[2 * i + 1], dhalf0)
            o_ref[0, (n - 1) * tq:n * tq, :] = out_t.T.astype(BF16)

    _grouped_tile_cases(step, _tile_groups(nb, MOBA_TILES_PER_STEP), tiles_case)


def _moba_call(qm, km, vm):
    b, s, w = qm.shape
    nb = s // min(MOBA_BLOCK, s)
    n_pairs = w // LANES
    seq = pl.BlockSpec((1, s, LANES), lambda bi, p, j: (bi, 0, p))
    return pl.pallas_call(
        _moba_kernel,
        grid=(b, n_pairs, len(_tile_groups(nb, MOBA_TILES_PER_STEP))),
        in_specs=[seq, seq, seq],
        out_specs=seq,
        out_shape=jax.ShapeDtypeStruct(qm.shape, BF16),
        scratch_shapes=[pltpu.VMEM((2, LANES, s), BF16), pltpu.VMEM((nb, LANES), F32)],
        compiler_params=_cparams("parallel", "parallel", "arbitrary"),
        name="moba",
    )(qm, km, vm)


def _merge_kernel(x_ref, sc_ref, sh_ref, gt_ref, g_ref, oa_ref, ob_ref, wga_ref, wgb_ref,
                  wbn_ref, wbm_ref, wo_ref, o_ref):
    x = x_ref[0]
    h = _adaln_norm(x, g_ref[...], sc_ref[...], sh_ref[...]).astype(BF16)
    ga = jax.nn.sigmoid(_dot(h, wga_ref[...]))
    gb = jax.nn.sigmoid(_dot(h, wgb_ref[...]))
    mixed = ga * _dot(oa_ref[0], wbn_ref[...]) + gb * _dot(ob_ref[0], wbm_ref[...])
    o_ref[0] = x + gt_ref[...] * _dot(mixed.astype(BF16), wo_ref[...])


def _merge_call(x, mod4, g_attn, oa, ob, wga, wgb, wbn, wbm, wo):
    b, s, d = x.shape
    tm = min(TM_PROJ, s)
    tok = lambda width: pl.BlockSpec((1, tm, width), lambda bi, i: (bi, i, 0))
    modspec = lambda k: pl.BlockSpec((None, None, 1, d), lambda bi, i: (bi, k, 0, 0))
    return pl.pallas_call(
        _merge_kernel,
        grid=(b, s // tm),
        in_specs=[tok(d), modspec(1), modspec(0), modspec(2), _const_spec((1, d)),
                  tok(oa.shape[2]), tok(ob.shape[2]), _const_spec(wga.shape),
                  _const_spec(wgb.shape), _const_spec(wbn.shape), _const_spec(wbm.shape),
                  _const_spec(wo.shape)],
        out_specs=tok(d),
        out_shape=jax.ShapeDtypeStruct(x.shape, F32),
        compiler_params=_cparams("parallel", "parallel"),
        name="merge",
    )(x, mod4, mod4, mod4, g_attn, oa, ob, wga, wgb, wbn, wbm, wo)


HALO = 8


def _ffn_kernel(x_ref, halo_ref, sc_ref, sh_ref, gt_ref, g_ref, wa_ref, wv_ref, cw_ref, cb_ref,
                wd_ref, o_ref):
    x = x_ref[0]
    tm = x.shape[0]
    g = g_ref[...]
    h = _adaln_norm(x, g, sc_ref[...], sh_ref[...]).astype(BF16)
    h_halo = _adaln_norm(halo_ref[0], g, sc_ref[...], sh_ref[...]).astype(BF16)
    h_ext = jnp.concatenate([h_halo, h], axis=0)
    ext_row = lax.broadcasted_iota(jnp.int32, (HALO + tm, FF_CHUNK), 0)
    live = ext_row >= jnp.where(pl.program_id(1) > 0, 0, HALO)

    def up(c):
        return jnp.where(live, _dot(h_ext, wa_ref[c]), 0.0), _dot(h, wv_ref[c])

    acc = None
    n_chunks = wa_ref.shape[0]
    ups = [up(c) for c in range(n_chunks)]
    for c in range(n_chunks):
        a, v = ups[c]
        cw = cw_ref[c]
        y = cb_ref[c] + cw[CONV_WIDTH - 1:CONV_WIDTH, :] * a[HALO:, :]
        for back in range(1, CONV_WIDTH):
            k = CONV_WIDTH - 1 - back
            y = y + cw[k:k + 1, :] * pltpu.roll(a, back, 0)[HALO:, :]
        gated = jax.nn.gelu(y) * v
        part = _dot(gated.astype(BF16), wd_ref[c])
        acc = part if acc is None else acc + part
    o_ref[0] = x + gt_ref[...] * acc


def _ffn_call(x, mod4, g_ffn, wa, wv, cw, cb, wd):
    b, s, d = x.shape
    tm = min(TM_PROJ, s)
    tok = pl.BlockSpec((1, tm, d), lambda bi, i: (bi, i, 0))
    halo = pl.BlockSpec((1, HALO, d), lambda bi, i: (bi, jnp.maximum(i * (tm // HALO) - 1, 0), 0))
    modspec = lambda k: pl.BlockSpec((None, None, 1, d), lambda bi, i: (bi, k, 0, 0))
    return pl.pallas_call(
        _ffn_kernel,
        grid=(b, s // tm),
        in_specs=[tok, halo, modspec(4), modspec(3), modspec(5), _const_spec((1, d)),
                  _const_spec(wa.shape), _const_spec(wv.shape), _const_spec(cw.shape),
                  _const_spec(cb.shape), _const_spec(wd.shape)],
        out_specs=tok,
        out_shape=jax.ShapeDtypeStruct(x.shape, F32),
        compiler_params=_cparams("parallel", "parallel"),
        name="ffn",
    )(x, x, mod4, mod4, mod4, g_ffn, wa, wv, cw, cb, wd)


def _block_diag_mean():
    lane = np.arange(LANES)
    return (lane[:, None] // HEAD_DIM == lane[None, :] // HEAD_DIM).astype(np.float32) / HEAD_DIM


def _overlap_t(n_cmp, n_cmp_pad, n_sb):
    c_start = np.arange(n_cmp)[:, None] * NSA_CMP_STRIDE
    js = np.arange(n_sb)[None, :]
    ov = ((c_start < (js + 1) * NSA_SEL_BLOCK) & (c_start + NSA_CMP_BLOCK > js * NSA_SEL_BLOCK))
    out = np.zeros((n_sb, n_cmp_pad), np.float32)
    out[:, :n_cmp] = ov.T
    return out


def _gate_expand_t(branch):
    out = np.zeros((NSA_WIDTH, LANES), np.float32)
    for slot, head in enumerate(NSA_HEAD_ORDER):
        out[slot * HEAD_DIM:(slot + 1) * HEAD_DIM, 3 * head + branch] = 1.0
    return out


def _block_diag2(w):
    z = jnp.zeros_like(w)
    return jnp.concatenate([jnp.concatenate([w, z], axis=1), jnp.concatenate([z, w], axis=1)], axis=0)


def _layer(x, c, positions, w_ada, b_ada, g_attn_norm, w_in, g_q_nsa, g_k_cmp, g_k_slc, g_k_win,
           cmp_k_pos, cmp_k_w1, cmp_k_w2, cmp_v_pos, cmp_v_w1, cmp_v_w2, g_q_moba, g_k_moba,
           w_branch_nsa, w_branch_moba, w_out, g_ffn_norm, w_ffn_up, conv_w, conv_b, w_ffn_down):
    b, s, d = x.shape
    order = np.asarray(NSA_HEAD_ORDER)
    off = IN_OFFSETS

    mod = _mod_call(c, w_ada, b_ada)
    mod4 = mod.reshape(b, 6, 1, d)

    col = lambda k: w_in[:, off[k]:off[k + 1]]
    w_qn = col(0).reshape(d, NSA_HEADS, HEAD_DIM)[:, order].reshape(d, NSA_WIDTH)
    w_gn = jnp.pad(col(7), ((0, 0), (0, LANES - 3 * NSA_HEADS)))
    w_p = jnp.concatenate([w_qn, col(8), col(9), col(10), col(1), col(2), col(3), col(4), col(5),
                           col(6), w_gn], axis=1).astype(BF16)
    tile2 = lambda gv: jnp.tile(gv, LANES // HEAD_DIM)
    gains = jnp.stack([tile2(g_q_nsa), tile2(g_q_moba), tile2(g_k_moba), tile2(g_k_slc),
                       tile2(g_k_win)] + [jnp.ones((LANES,), F32)] * 3)
    bd = jnp.asarray(_block_diag_mean(), BF16)
    half = HEAD_DIM // 2
    inv_freq = ROPE_THETA ** (-jnp.arange(half, dtype=F32) / half)
    inv_freq = jnp.tile(inv_freq, LANES // half).reshape(1, LANES)

    (qn, qm, km, vm, kc, vc, ksl, vsl, kwn, vwn, gn) = _inproj_call(
        x, mod4, g_attn_norm.reshape(1, d), positions.reshape(b, s, 1), inv_freq, w_p, gains, bd)

    n_grp = s // NSA_CMP_STRIDE
    n_cmp = (s - NSA_CMP_BLOCK) // NSA_CMP_STRIDE + 1
    w1_pack = lambda w1: jax.vmap(_block_diag2)(
        w1.reshape(NSA_CMP_BLOCK, HEAD_DIM, HEAD_DIM)).astype(BF16)
    pos_pack = lambda p: jnp.tile(p, (1, NSA_KV_HEADS))
    kcmp, vcmp = _compress_call(
        kc, vc, pos_pack(cmp_k_pos), pos_pack(cmp_v_pos),
        w1_pack(cmp_k_w1), w1_pack(cmp_v_w1), _block_diag2(cmp_k_w2).astype(BF16),
        _block_diag2(cmp_v_w2).astype(BF16), tile2(g_k_cmp).reshape(1, LANES), bd)

    n_sb = s // NSA_SEL_BLOCK
    ovt = jnp.asarray(_overlap_t(n_cmp, n_grp, n_sb), BF16)
    ocmpt, selt = _nsa_cmp_call(qn, gn, kcmp, vcmp, ovt, jnp.asarray(_gate_expand_t(0), BF16))
    oa = _nsa_attn_call(qn, selt, gn, ocmpt, ksl, vsl, kwn, vwn,
                        jnp.asarray(_gate_expand_t(1), BF16), jnp.asarray(_gate_expand_t(2), BF16))

    ob = _moba_call(qm, km, vm)

    w_bn = w_branch_nsa.reshape(NSA_HEADS, HEAD_DIM, d)[order].reshape(NSA_WIDTH, d)
    x1 = _merge_call(x, mod4, g_attn_norm.reshape(1, d), oa, ob, col(11).astype(BF16),
                     col(12).astype(BF16), w_bn.astype(BF16), w_branch_moba.astype(BF16),
                     w_out.astype(BF16))

    chunks = lambda w: w.reshape(w.shape[0], N_FF_CHUNKS, FF_CHUNK).transpose(1, 0, 2)
    wa = chunks(w_ffn_up[:, :D_FF]).astype(BF16)
    wv = chunks(w_ffn_up[:, D_FF:]).astype(BF16)
    cw = chunks(conv_w)
    cb = chunks(conv_b.reshape(1, D_FF))
    wd = w_ffn_down.reshape(N_FF_CHUNKS, FF_CHUNK, d).astype(BF16)
    return _ffn_call(x1, mod4, g_ffn_norm.reshape(1, d), wa, wv, cw, cb, wd)


def kernel(x, c, positions, w_ada, b_ada, g_attn_norm, w_in, g_q_nsa, g_k_cmp, g_k_slc, g_k_win,
           cmp_k_pos, cmp_k_w1, cmp_k_w2, cmp_v_pos, cmp_v_w1, cmp_v_w2, g_q_moba, g_k_moba,
           w_branch_nsa, w_branch_moba, w_out, g_ffn_norm, w_ffn_up, conv_w, conv_b, w_ffn_down):
    for l in range(w_ada.shape[0]):
        x = _layer(x, c, positions, w_ada[l], b_ada[l], g_attn_norm[l], w_in[l], g_q_nsa[l],
                   g_k_cmp[l], g_k_slc[l], g_k_win[l], cmp_k_pos[l], cmp_k_w1[l], cmp_k_w2[l],
                   cmp_v_pos[l], cmp_v_w1[l], cmp_v_w2[l], g_q_moba[l], g_k_moba[l],
                   w_branch_nsa[l], w_branch_moba[l], w_out[l], g_ffn_norm[l], w_ffn_up[l],
                   conv_w[l], conv_b[l], w_ffn_down[l])
    return x
```

```python
import numpy as np
import jax
import jax.numpy as jnp
from jax import lax
from jax.experimental import pallas as pl
from jax.experimental.pallas import tpu as pltpu

F32 = jnp.float32
BF16 = jnp.bfloat16

D_MODEL = 1024
HEAD_DIM = 64
NSA_HEADS = 8
NSA_KV_HEADS = 2
NSA_CMP_BLOCK = 32
NSA_CMP_STRIDE = 16
NSA_SEL_BLOCK = 64
NSA_SEL_TOPN = 16
NSA_WINDOW = 512
NSA_FORCE_BONUS = 1e4
MOBA_HEADS = 8
MOBA_BLOCK = 256
MOBA_TOPK = 3
CONV_WIDTH = 3
ROPE_THETA = 10000.0
NORM_EPS = 1e-6
NEG_INF = -1e30

LANES = 128
LOG2_E = 1.4426950408889634
SCALE = HEAD_DIM ** -0.5 * LOG2_E
MASK_BIAS = -(2.0 ** 100)
VMEM_LIMIT = 56 * 1024 * 1024

TM_PROJ = 512
TM_INPROJ = 1024
TQ = 256
FF_CHUNK = 256

NSA_WIDTH = NSA_HEADS * HEAD_DIM
MOBA_WIDTH = MOBA_HEADS * HEAD_DIM
KV_WIDTH = NSA_KV_HEADS * HEAD_DIM
IN_SIZES = (NSA_WIDTH, KV_WIDTH, KV_WIDTH, KV_WIDTH, KV_WIDTH, KV_WIDTH, KV_WIDTH,
            3 * NSA_HEADS, MOBA_WIDTH, MOBA_WIDTH, MOBA_WIDTH, D_MODEL, D_MODEL)
IN_OFFSETS = np.concatenate([[0], np.cumsum(IN_SIZES)]).tolist()

NSA_HEAD_ORDER = (0, 4, 1, 5, 2, 6, 3, 7)


def _dot(a, b):
    return jnp.dot(a, b, preferred_element_type=F32)


def _dot_nt(a, b):
    return lax.dot_general(a, b, (((1,), (1,)), ((), ())), preferred_element_type=F32)


def _split_bf16(v):
    hi = v.astype(BF16)
    lo = (v - hi.astype(F32)).astype(BF16)
    return hi, lo


def _cparams(*sem, flags=None):
    return pltpu.CompilerParams(dimension_semantics=sem, vmem_limit_bytes=VMEM_LIMIT, flags=flags)


def _const_spec(shape):
    n = len(shape)
    return pl.BlockSpec(shape, lambda *_: (0,) * n, pipeline_mode=pl.Buffered(1))


def _adaln_norm(x, g, sc, sh):
    y = x * lax.rsqrt(jnp.mean(x * x, axis=-1, keepdims=True) + NORM_EPS)
    return (y * g) * (1.0 + sc) + sh


def _mod_kernel(c_ref, w_ref, b_ref, o_ref):
    o_ref[...] = jnp.dot(c_ref[...], w_ref[...], preferred_element_type=F32,
                         precision=lax.Precision.HIGHEST) + b_ref[...]


def _mod_call(c, w_ada, b_ada):
    b, d = c.shape
    n = w_ada.shape[1]
    tn = D_MODEL
    return pl.pallas_call(
        _mod_kernel,
        grid=(n // tn,),
        in_specs=[pl.BlockSpec((b, d), lambda j: (0, 0)),
                  pl.BlockSpec((d, tn), lambda j: (0, j)),
                  pl.BlockSpec((1, tn), lambda j: (0, j))],
        out_specs=pl.BlockSpec((b, tn), lambda j: (0, j)),
        out_shape=jax.ShapeDtypeStruct((b, n), F32),
        compiler_params=_cparams("parallel"),
        name="mod",
    )(c, w_ada, b_ada.reshape(1, n))


_P_QN, _P_QM, _P_KM, _P_VM = 0, 512, 1024, 1536
_P_KC, _P_VC, _P_KSL, _P_VSL, _P_KWN, _P_VWN, _P_GN = 2048, 2176, 2304, 2432, 2560, 2688, 2816
_P_WIDTH = 2944
ROPE_FREQS = HEAD_DIM // 2
ROPE_PACK = LANES // ROPE_FREQS


def _inproj_kernel(x_ref, sc_ref, sh_ref, g_ref, pos_ref, invf_ref, w_ref, gains_ref, bd_ref,
                   qn_ref, qm_ref, km_ref, vm_ref, kc_ref, vc_ref, ksl_ref, vsl_ref, kwn_ref,
                   vwn_ref, gn_ref, cos_ref, sin_ref):
    x = x_ref[0]
    tm = x.shape[0]
    h = _adaln_norm(x, g_ref[...], sc_ref[...], sh_ref[...]).astype(BF16)

    ang = pos_ref[0].astype(F32) * invf_ref[...]
    cos_p = jnp.cos(ang)
    sin_p = jnp.sin(ang)
    group = lax.broadcasted_iota(jnp.int32, ang.shape, 1) // ROPE_FREQS
    for k in range(ROPE_PACK):
        for packed, dst in ((cos_p, cos_ref), (sin_p, sin_ref)):
            own = jnp.where(group == k, packed, 0.0)
            spread = own
            for shift in range(1, ROPE_PACK):
                spread = spread + pltpu.roll(own, shift * ROPE_FREQS, 1)
            dst[pl.ds(k, tm // ROPE_PACK, stride=ROPE_PACK), :] = spread
    cos = cos_ref[...]
    sin = sin_ref[...]
    lane = lax.broadcasted_iota(jnp.int32, (tm, LANES), 1)
    first = (lane & (HEAD_DIM // 2)) == 0
    sin_signed = jnp.where(first, -sin, sin)
    bd = bd_ref[...]

    def head_norm(y, gain):
        ms = _dot((y * y).astype(BF16), bd)
        return y * lax.rsqrt(ms + NORM_EPS) * gain

    def rope(y):
        partner = jnp.where(first, pltpu.roll(y, LANES - HEAD_DIM // 2, 1),
                            pltpu.roll(y, HEAD_DIM // 2, 1))
        return y * cos + partner * sin_signed

    def proj(off, width):
        return _dot(h, w_ref[:, off:off + width])

    def wide(off, out_ref, gain_row, scale):
        acc = proj(off, 4 * LANES)
        gain = gains_ref[gain_row:gain_row + 1, :]
        for p in range(4):
            y = rope(head_norm(acc[:, p * LANES:(p + 1) * LANES], gain))
            if scale != 1.0:
                y = y * scale
            out_ref[0, :, p * LANES:(p + 1) * LANES] = y.astype(BF16)

    wide(_P_QN, qn_ref, 0, SCALE)
    wide(_P_QM, qm_ref, 1, SCALE)
    wide(_P_KM, km_ref, 2, 1.0)
    vm_ref[0] = proj(_P_VM, 4 * LANES).astype(BF16)

    small = proj(_P_KC, 7 * LANES)
    kc_ref[0] = rope(small[:, 0:LANES])
    vc_ref[0] = small[:, LANES:2 * LANES]
    ksl_ref[0] = rope(head_norm(small[:, 2 * LANES:3 * LANES], gains_ref[3:4, :])).astype(BF16)
    vsl_ref[0] = small[:, 3 * LANES:4 * LANES].astype(BF16)
    kwn_ref[0] = rope(head_norm(small[:, 4 * LANES:5 * LANES], gains_ref[4:5, :])).astype(BF16)
    vwn_ref[0] = small[:, 5 * LANES:6 * LANES].astype(BF16)
    gn_ref[0] = jax.nn.sigmoid(small[:, 6 * LANES:7 * LANES])


def _inproj_call(x, mod4, g_attn, pos_packed, inv_freq, w_p, gains, bd):
    b, s, d = x.shape
    tm = min(TM_INPROJ, s)
    tok = lambda width: pl.BlockSpec((1, tm, width), lambda bi, i: (bi, i, 0))
    modspec = lambda k: pl.BlockSpec((None, None, 1, d), lambda bi, i: (bi, k, 0, 0))
    shapes = [jax.ShapeDtypeStruct((b, s, 4 * LANES), BF16)] * 4 \
        + [jax.ShapeDtypeStruct((b, s, LANES), F32)] * 2 \
        + [jax.ShapeDtypeStruct((b, s, LANES), BF16)] * 4 \
        + [jax.ShapeDtypeStruct((b, s, LANES), F32)]
    return pl.pallas_call(
        _inproj_kernel,
        grid=(b, s // tm),
        in_specs=[tok(d), modspec(1), modspec(0), _const_spec((1, d)),
                  pl.BlockSpec((1, tm // ROPE_PACK, LANES), lambda bi, i: (bi, i, 0)),
                  _const_spec((1, LANES)), _const_spec((d, _P_WIDTH)),
                  _const_spec((8, LANES)), _const_spec((LANES, LANES))],
        out_specs=[tok(4 * LANES)] * 4 + [tok(LANES)] * 7,
        out_shape=shapes,
        scratch_shapes=[pltpu.VMEM((tm, LANES), F32)] * 2,
        compiler_params=_cparams("parallel", "parallel"),
        name="inproj",
    )(x, mod4, mod4, g_attn, pos_packed, inv_freq, w_p, gains, bd)


def _compress_kernel(xk_ref, xv_ref, posk_ref, posv_ref, w1k_ref, w1v_ref, w2k_ref, w2v_ref,
                     gain_ref, bd_ref, ko_ref, vo_ref):
    n_grp = ko_ref.shape[1]

    def mlp(x_ref, pos_ref, w1_ref, w2_ref):
        first = None
        second = None
        for l in range(NSA_CMP_STRIDE):
            xl = x_ref[0, pl.ds(l, n_grp, stride=NSA_CMP_STRIDE), :]
            a = _dot((xl + pos_ref[l:l + 1, :]).astype(BF16), w1_ref[l])
            lb = NSA_CMP_STRIDE + l
            b2 = _dot((xl + pos_ref[lb:lb + 1, :]).astype(BF16), w1_ref[lb])
            first = a if first is None else first + a
            second = b2 if second is None else second + b2
        hidden = first + pltpu.roll(second, n_grp - 1, 0)
        return _dot(jax.nn.gelu(hidden).astype(BF16), w2_ref[...])

    k = mlp(xk_ref, posk_ref, w1k_ref, w2k_ref)
    ms = _dot((k * k).astype(BF16), bd_ref[...])
    ko_ref[0] = (k * lax.rsqrt(ms + NORM_EPS) * gain_ref[...]).astype(BF16)
    vo_ref[0] = mlp(xv_ref, posv_ref, w1v_ref, w2v_ref).astype(BF16)


def _compress_call(xk, xv, posk, posv, w1k, w1v, w2k, w2v, gain, bd):
    b, s, w = xk.shape
    n = s // NSA_CMP_STRIDE
    xspec = pl.BlockSpec((1, s, w), lambda bi: (bi, 0, 0))
    ospec = pl.BlockSpec((1, n, LANES), lambda bi: (bi, 0, 0))
    return pl.pallas_call(
        _compress_kernel,
        grid=(b,),
        in_specs=[xspec, xspec, _const_spec(posk.shape), _const_spec(posv.shape),
                  _const_spec(w1k.shape), _const_spec(w1v.shape), _const_spec(w2k.shape),
                  _const_spec(w2v.shape), _const_spec((1, LANES)), _const_spec((LANES, LANES))],
        out_specs=[ospec, ospec],
        out_shape=[jax.ShapeDtypeStruct((b, n, LANES), BF16)] * 2,
        compiler_params=_cparams("parallel"),
        name="compress",
    )(xk, xv, posk, posv, w1k, w1v, w2k, w2v, gain, bd)


def _lane_half0(tq):
    return lax.broadcasted_iota(jnp.int32, (tq, LANES), 1) < HEAD_DIM


def _dim_half0(tq):
    return lax.broadcasted_iota(jnp.int32, (LANES, tq), 0) < HEAD_DIM


def _take_half(q2, half0, g):
    qf = q2.astype(F32)
    keep = half0 if g == 0 else jnp.logical_not(half0)
    return jnp.where(keep, qf, 0.0).astype(BF16)


def _transposed_bf16(rows):
    return rows.astype(F32).T.astype(BF16)


def _rank_counts(vals, jb, n):
    counts = jnp.zeros_like(vals)
    for j in range(n):
        row = vals[j:j + 1, :]
        beats = jnp.where(vals > row, 1.0,
                          jnp.where(vals == row, jnp.where(jb < j, 1.0, 0.0), 0.0))
        cnt = jnp.sum(beats, axis=0, keepdims=True)
        counts = jnp.where(jb == j, cnt, counts)
    return counts


def _values_t_with_ones(rows, dhalf0):
    vt = rows.astype(F32).T
    return (jnp.where(dhalf0, vt, 1.0).astype(BF16), jnp.where(dhalf0, 1.0, vt).astype(BF16))


def _softmax_max(parts):
    m = None
    for s, _, bias in parts:
        if bias is None:
            mx = jnp.max(s, axis=0, keepdims=True)
        else:
            n_blk = bias.shape[0]
            blk = s.shape[0] // n_blk
            mx = None
            for b in range(n_blk):
                mb = jnp.max(s[b * blk:(b + 1) * blk], axis=0, keepdims=True) + bias[b:b + 1]
                mx = mb if mx is None else jnp.maximum(mx, mb)
        m = mx if m is None else jnp.maximum(m, mx)
    return m


def _softmax_values(parts, m):
    acc = None
    for s, vt, bias in parts:
        if bias is None:
            p = jnp.exp2(s - m)
        else:
            n_blk = bias.shape[0]
            blk = s.shape[0] // n_blk
            p = jnp.concatenate([jnp.exp2(s[b * blk:(b + 1) * blk] - (m - bias[b:b + 1]))
                                 for b in range(n_blk)], axis=0)
        pv = _dot(vt, p.astype(BF16))
        acc = pv if acc is None else acc + pv
    return acc


def _solve_all(problems):
    all_parts = [problem() for problem in problems]
    maxima = [_softmax_max(parts) for parts in all_parts]
    return [_softmax_values(parts, m) for parts, m in zip(all_parts, maxima)]


def _normalized_pair(acc0, acc1, dhalf0):
    l0 = jnp.maximum(acc0[HEAD_DIM:HEAD_DIM + 1, :], 1e-30)
    l1 = jnp.maximum(acc1[0:1, :], 1e-30)
    return jnp.where(dhalf0, acc0 * (1.0 / l0), acc1 * (1.0 / l1))


NSA_TILES_PER_STEP = 4
MOBA_TILES_PER_STEP = 8


def _tile_groups(n_tiles, tiles_per_step):
    pairs = [[j + 1] if n_tiles - j == j + 1 else [j + 1, n_tiles - j]
             for j in range((n_tiles + 1) // 2)]
    per_group = max(tiles_per_step // 2, 1)
    return [sum(pairs[i:i + per_group], []) for i in range(0, len(pairs), per_group)]


def _grouped_tile_cases(step, groups, tiles_case):
    for j, tiles in enumerate(groups):
        pl.when(step == j)(lambda tiles=tiles: tiles_case(tiles))


def _nsa_cmp_kernel(q_ref, gate_ref, kc_ref, vc_ref, ovt_ref, egt_ref, ocmpt_ref, selt_ref):
    tq = q_ref.shape[1]
    n_c = kc_ref.shape[1]
    t0 = pl.program_id(1) * tq
    g_hi, g_lo = _split_bf16(gate_ref[0])

    c_row = lax.broadcasted_iota(jnp.int32, (n_c, tq), 0)
    t_col = t0 + lax.broadcasted_iota(jnp.int32, (n_c, tq), 1)
    visible = c_row * NSA_CMP_STRIDE + (NSA_CMP_BLOCK - 1) <= t_col
    half0 = _lane_half0(tq)
    dhalf0 = _dim_half0(tq)
    kc = kc_ref[0]
    vct = _transposed_bf16(vc_ref[0])

    n_pairs = NSA_HEADS // NSA_KV_HEADS
    scores = [[_dot_nt(kc, _take_half(q_ref[0, :, r * LANES:(r + 1) * LANES], half0, g))
               for g in range(NSA_KV_HEADS)] for r in range(n_pairs)]
    psum = [None, None]
    for r in range(n_pairs):
        o_pair = None
        for g in range(NSA_KV_HEADS):
            s = jnp.where(visible, scores[r][g], NEG_INF)
            m = jnp.max(s, axis=0, keepdims=True)
            p = jnp.where(visible, jnp.exp2(s - m), 0.0)
            p = p / jnp.maximum(jnp.sum(p, axis=0, keepdims=True), 1e-30)
            psum[g] = p if psum[g] is None else psum[g] + p
            o = _dot(vct, p.astype(BF16))
            o_pair = o if g == 0 else jnp.where(dhalf0, o_pair, o)
        egt = egt_ref[r * LANES:(r + 1) * LANES, :]
        gate_t = _dot_nt(egt, g_hi) + _dot_nt(egt, g_lo)
        ocmpt_ref[0, r] = gate_t * o_pair

    n_sb = ovt_ref.shape[0]
    n_top = min(NSA_SEL_TOPN, n_sb)
    few_candidates = (t0 + tq) // NSA_SEL_BLOCK <= n_top

    @pl.when(few_candidates)
    def _():
        selt_ref[...] = jnp.zeros_like(selt_ref)

    @pl.when(jnp.logical_not(few_candidates))
    def _():
        jb = lax.broadcasted_iota(jnp.int32, (n_sb, tq), 0)
        own = (t0 + lax.broadcasted_iota(jnp.int32, (n_sb, tq), 1)) // NSA_SEL_BLOCK
        forced = (jb == 0) | (jb == own) | (jb == own - 1)
        ovt = ovt_ref[...]
        for g in range(NSA_KV_HEADS):
            p_hi, p_lo = _split_bf16(psum[g])
            imp = _dot(ovt, p_hi) + _dot(ovt, p_lo)
            imp = jnp.where(jb <= own, imp + jnp.where(forced, NSA_FORCE_BONUS, 0.0), NEG_INF)
            counts = _rank_counts(imp, jb, n_sb)
            selt_ref[0, g] = jnp.where(counts < float(n_top), 0.0, MASK_BIAS)


def _nsa_cmp_call(qn, gn, kcmp, vcmp, ovt, eg_cmp_t):
    b, s, w = qn.shape
    tq = min(TQ, s)
    n_pairs = w // LANES
    n_sb = ovt.shape[0]
    tok = lambda width: pl.BlockSpec((1, tq, width), lambda bi, i: (bi, i, 0))
    cspec = pl.BlockSpec((1, kcmp.shape[1], LANES), lambda bi, i: (bi, 0, 0))
    return pl.pallas_call(
        _nsa_cmp_kernel,
        grid=(b, s // tq),
        in_specs=[tok(w), tok(LANES), cspec, cspec, _const_spec(ovt.shape),
                  _const_spec(eg_cmp_t.shape)],
        out_specs=[pl.BlockSpec((1, n_pairs, LANES, tq), lambda bi, i: (bi, 0, 0, i)),
                   pl.BlockSpec((1, NSA_KV_HEADS, n_sb, tq), lambda bi, i: (bi, 0, 0, i))],
        out_shape=[jax.ShapeDtypeStruct((b, n_pairs, LANES, s), F32),
                   jax.ShapeDtypeStruct((b, NSA_KV_HEADS, n_sb, s), F32)],
        compiler_params=_cparams("parallel", "parallel"),
        name="nsa_cmp",
    )(qn, gn, kcmp, vcmp, ovt, eg_cmp_t)


def _nsa_attn_kernel(q_ref, selt_ref, gate_ref, ocmpt_ref, ksl_ref, vsl_ref, kwn_ref, vwn_ref,
                     egst_ref, egwt_ref, o_ref, vslt_ref, vwnt_ref):
    s_len = q_ref.shape[1]
    tq = min(TQ, s_len)
    n_tiles = s_len // tq
    step = pl.program_id(2)
    half0 = _lane_half0(tq)
    dhalf0 = _dim_half0(tq)
    k_row = lax.broadcasted_iota(jnp.int32, (tq, tq), 0)
    q_col = lax.broadcasted_iota(jnp.int32, (tq, tq), 1)
    causal = k_row <= q_col
    blocks_per_chunk = tq // NSA_SEL_BLOCK

    @pl.when(jnp.logical_and(pl.program_id(1) == 0, step == 0))
    def _():
        for c in range(n_tiles):
            cols = slice(c * tq, (c + 1) * tq)
            vslt_ref[0, :, cols], vslt_ref[1, :, cols] = _values_t_with_ones(vsl_ref[0, cols, :],
                                                                             dhalf0)
            vwnt_ref[0, :, cols], vwnt_ref[1, :, cols] = _values_t_with_ones(vwn_ref[0, cols, :],
                                                                             dhalf0)

    def sel_problem(n, qh, g):
        lo = (n - 1) * tq
        all_selected = n * blocks_per_chunk <= NSA_SEL_TOPN
        parts = []
        for c in range(n):
            k0 = c * tq
            sc = _dot_nt(ksl_ref[0, k0:k0 + tq, :], qh)
            if c == n - 1:
                sc = jnp.where(causal, sc, NEG_INF)
            bias = None
            if not all_selected:
                j0 = c * blocks_per_chunk
                bias = selt_ref[0, g, j0:j0 + blocks_per_chunk, lo:lo + tq]
            parts.append((sc, vslt_ref[g, :, k0:k0 + tq], bias))
        return parts

    def win_problem(n, qh, g):
        lo = (n - 1) * tq
        parts = [(jnp.where(causal, _dot_nt(kwn_ref[0, lo:lo + tq, :], qh), NEG_INF),
                  vwnt_ref[g, :, lo:lo + tq], None)]
        for back in range(1, min(n - 1, (NSA_WINDOW + tq - 2) // tq) + 1):
            k0 = lo - back * tq
            sw = _dot_nt(kwn_ref[0, k0:k0 + tq, :], qh)
            if (back + 1) * tq - 1 >= NSA_WINDOW:
                sw = jnp.where(q_col - k_row + back * tq < NSA_WINDOW, sw, NEG_INF)
            parts.append((sw, vwnt_ref[g, :, k0:k0 + tq], None))
        return parts

    def tiles_case(tiles):
        problems = []
        for n in tiles:
            q2 = q_ref[0, (n - 1) * tq:n * tq, :]
            for g in range(NSA_KV_HEADS):
                qh = _take_half(q2, half0, g)
                problems.append(lambda n=n, qh=qh, g=g: sel_problem(n, qh, g))
                problems.append(lambda n=n, qh=qh, g=g: win_problem(n, qh, g))
        accs = _solve_all(problems)
        for i, n in enumerate(tiles):
            lo = (n - 1) * tq
            sel0, win0, sel1, win1 = accs[4 * i:4 * i + 4]
            g_hi, g_lo = _split_bf16(gate_ref[0, lo:lo + tq, :])
            gs_t = _dot_nt(egst_ref[...], g_hi) + _dot_nt(egst_ref[...], g_lo)
            gw_t = _dot_nt(egwt_ref[...], g_hi) + _dot_nt(egwt_ref[...], g_lo)
            out_t = (ocmpt_ref[0, 0, :, lo:lo + tq] + gs_t * _normalized_pair(sel0, sel1, dhalf0)
                     + gw_t * _normalized_pair(win0, win1, dhalf0))
            o_ref[0, lo:lo + tq, :] = out_t.T.astype(BF16)

    _grouped_tile_cases(step, _tile_groups(n_tiles, NSA_TILES_PER_STEP), tiles_case)


def _nsa_attn_call(qn, selt, gn, ocmpt, ksl, vsl, kwn, vwn, eg_sel_t, eg_win_t):
    b, s, _ = qn.shape
    n_tiles = s // min(TQ, s)
    n_pairs = qn.shape[2] // LANES
    pair = pl.BlockSpec((1, s, LANES), lambda bi, r, j: (bi, 0, r))
    seq = pl.BlockSpec((1, s, LANES), lambda bi, r, j: (bi, 0, 0))
    egspec = pl.BlockSpec((LANES, LANES), lambda bi, r, j: (r, 0))
    return pl.pallas_call(
        _nsa_attn_kernel,
        grid=(b, n_pairs, len(_tile_groups(n_tiles, NSA_TILES_PER_STEP))),
        in_specs=[pair,
                  pl.BlockSpec((1,) + selt.shape[1:], lambda bi, r, j: (bi, 0, 0, 0)),
                  seq,
                  pl.BlockSpec((1, 1, LANES, s), lambda bi, r, j: (bi, r, 0, 0)),
                  seq, seq, seq, seq, egspec, egspec],
        out_specs=pair,
        out_shape=jax.ShapeDtypeStruct(qn.shape, BF16),
        scratch_shapes=[pltpu.VMEM((NSA_KV_HEADS, LANES, s), BF16)] * 2,
        compiler_params=_cparams("parallel", "arbitrary", "arbitrary"),
        name="nsa_attn",
    )(qn, selt, gn, ocmpt, ksl, vsl, kwn, vwn, eg_sel_t, eg_win_t)


def _moba_kernel(q_ref, k_ref, v_ref, o_ref, vt_ref, kmean_ref):
    s_len = q_ref.shape[1]
    tq = min(MOBA_BLOCK, s_len)
    nb = s_len // tq
    n_top = min(MOBA_TOPK, nb - 1)
    step = pl.program_id(2)
    half0 = _lane_half0(tq)
    dhalf0 = _dim_half0(tq)
    k_row = lax.broadcasted_iota(jnp.int32, (tq, tq), 0)
    q_col = lax.broadcasted_iota(jnp.int32, (tq, tq), 1)
    causal = k_row <= q_col
    jb = lax.broadcasted_iota(jnp.int32, (nb, tq), 0)

    @pl.when(step == 0)
    def _():
        for j in range(nb):
            rows = slice(j * tq, (j + 1) * tq)
            kmean_ref[j:j + 1, :] = jnp.mean(k_ref[0, rows, :].astype(F32), axis=0, keepdims=True)
            vt_ref[0, :, rows], vt_ref[1, :, rows] = _values_t_with_ones(v_ref[0, rows, :], dhalf0)

    def problem(n, qh, h):
        lo = (n - 1) * tq
        n_past = n - 1
        bias = None
        if n_past > n_top:
            km_hi, km_lo = _split_bf16(kmean_ref[...])
            gate = _dot_nt(km_hi, qh) + _dot_nt(km_lo, qh)
            gate = jnp.where(jb < n_past, gate, NEG_INF)
            counts = _rank_counts(gate, jb, n_past)
            bias = jnp.where(counts < float(n_top), 0.0, MASK_BIAS)
        parts = []
        for j in range(n_past):
            sj = _dot_nt(k_ref[0, j * tq:(j + 1) * tq, :], qh)
            parts.append((sj, vt_ref[h, :, j * tq:(j + 1) * tq],
                          None if bias is None else bias[j:j + 1, :]))
        s_own = _dot_nt(k_ref[0, lo:lo + tq, :], qh)
        parts.append((jnp.where(causal, s_own, NEG_INF), vt_ref[h, :, lo:lo + tq], None))
        return parts

    def tiles_case(tiles):
        problems = []
        for n in tiles:
            q2 = q_ref[0, (n - 1) * tq:n * tq, :]
            for h in range(2):
                qh = _take_half(q2, half0, h)
                problems.append(lambda n=n, qh=qh, h=h: problem(n, qh, h))
        accs = _solve_all(problems)
        for i, n in enumerate(tiles):
            out_t = _normalized_pair(accs[2 * i], accs[2 * i + 1], dhalf0)
            o_ref[0, (n - 1) * tq:n * tq, :] = out_t.T.astype(BF16)

    _grouped_tile_cases(step, _tile_groups(nb, MOBA_TILES_PER_STEP), tiles_case)


def _moba_call(qm, km, vm):
    b, s, w = qm.shape
    nb = s // min(MOBA_BLOCK, s)
    n_pairs = w // LANES
    seq = pl.BlockSpec((1, s, LANES), lambda bi, p, j: (bi, 0, p))
    return pl.pallas_call(
        _moba_kernel,
        grid=(b, n_pairs, len(_tile_groups(nb, MOBA_TILES_PER_STEP))),
        in_specs=[seq, seq, seq],
        out_specs=seq,
        out_shape=jax.ShapeDtypeStruct(qm.shape, BF16),
        scratch_shapes=[pltpu.VMEM((2, LANES, s), BF16), pltpu.VMEM((nb, LANES), F32)],
        compiler_params=_cparams("parallel", "parallel", "arbitrary"),
        name="moba",
    )(qm, km, vm)


def _merge_kernel(x_ref, sc_ref, sh_ref, gt_ref, g_ref, oa_ref, ob_ref, wga_ref, wgb_ref,
                  wbn_ref, wbm_ref, wo_ref, o_ref):
    x = x_ref[0]
    h = _adaln_norm(x, g_ref[...], sc_ref[...], sh_ref[...]).astype(BF16)
    ga = jax.nn.sigmoid(_dot(h, wga_ref[...]))
    gb = jax.nn.sigmoid(_dot(h, wgb_ref[...]))
    mixed = ga * _dot(oa_ref[0], wbn_ref[...]) + gb * _dot(ob_ref[0], wbm_ref[...])
    o_ref[0] = x + gt_ref[...] * _dot(mixed.astype(BF16), wo_ref[...])


def _merge_call(x, mod4, g_attn, oa, ob, wga, wgb, wbn, wbm, wo):
    b, s, d = x.shape
    tm = min(TM_PROJ, s)
    tok = lambda width: pl.BlockSpec((1, tm, width), lambda bi, i: (bi, i, 0))
    modspec = lambda k: pl.BlockSpec((None, None, 1, d), lambda bi, i: (bi, k, 0, 0))
    return pl.pallas_call(
        _merge_kernel,
        grid=(b, s // tm),
        in_specs=[tok(d), modspec(1), modspec(0), modspec(2), _const_spec((1, d)),
                  tok(oa.shape[2]), tok(ob.shape[2]), _const_spec(wga.shape),
                  _const_spec(wgb.shape), _const_spec(wbn.shape), _const_spec(wbm.shape),
                  _const_spec(wo.shape)],
        out_specs=tok(d),
        out_shape=jax.ShapeDtypeStruct(x.shape, F32),
        compiler_params=_cparams("parallel", "parallel"),
        name="merge",
    )(x, mod4, mod4, mod4, g_attn, oa, ob, wga, wgb, wbn, wbm, wo)


HALO = 8


def _ffn_kernel(x_ref, halo_ref, sc_ref, sh_ref, gt_ref, g_ref, wup_ref, cw_ref, cb_ref,
                wd_ref, o_ref):
    x = x_ref[0]
    tm = x.shape[0]
    g = g_ref[...]
    h = _adaln_norm(x, g, sc_ref[...], sh_ref[...]).astype(BF16)
    h_halo = _adaln_norm(halo_ref[0], g, sc_ref[...], sh_ref[...]).astype(BF16)
    h_ext = jnp.concatenate([h_halo, h], axis=0)
    ext_row = lax.broadcasted_iota(jnp.int32, (HALO + tm, FF_CHUNK), 0)
    live = ext_row >= jnp.where(pl.program_id(1) > 0, 0, HALO)

    d_ff = wd_ref.shape[0]
    n_chunks = d_ff // FF_CHUNK
    cols = lambda c: slice(c * FF_CHUNK, (c + 1) * FF_CHUNK)

    def up(c):
        a = _dot(h_ext, wup_ref[:, cols(c)])
        v = _dot(h, wup_ref[:, d_ff + c * FF_CHUNK:d_ff + (c + 1) * FF_CHUNK])
        return jnp.where(live, a, 0.0), v

    acc = None
    ups = [up(c) for c in range(n_chunks)]
    for c in range(n_chunks):
        a, v = ups[c]
        cw = cw_ref[:, cols(c)]
        y = cb_ref[:, cols(c)] + cw[CONV_WIDTH - 1:CONV_WIDTH, :] * a[HALO:, :]
        for back in range(1, CONV_WIDTH):
            k = CONV_WIDTH - 1 - back
            y = y + cw[k:k + 1, :] * pltpu.roll(a, back, 0)[HALO:, :]
        gated = jax.nn.gelu(y) * v
        part = _dot(gated.astype(BF16), wd_ref[cols(c), :])
        acc = part if acc is None else acc + part
    o_ref[0] = x + gt_ref[...] * acc


def _ffn_call(x, mod4, g_ffn, wup, cw, cb, wd):
    b, s, d = x.shape
    tm = min(TM_PROJ, s)
    tok = pl.BlockSpec((1, tm, d), lambda bi, i: (bi, i, 0))
    halo = pl.BlockSpec((1, HALO, d), lambda bi, i: (bi, jnp.maximum(i * (tm // HALO) - 1, 0), 0))
    modspec = lambda k: pl.BlockSpec((None, None, 1, d), lambda bi, i: (bi, k, 0, 0))
    return pl.pallas_call(
        _ffn_kernel,
        grid=(b, s // tm),
        in_specs=[tok, halo, modspec(4), modspec(3), modspec(5), _const_spec((1, d)),
                  _const_spec(wup.shape), _const_spec(cw.shape), _const_spec(cb.shape),
                  _const_spec(wd.shape)],
        out_specs=tok,
        out_shape=jax.ShapeDtypeStruct(x.shape, F32),
        compiler_params=_cparams("parallel", "parallel"),
        name="ffn",
    )(x, x, mod4, mod4, mod4, g_ffn, wup, cw, cb, wd)


def _block_diag_mean():
    lane = np.arange(LANES)
    return (lane[:, None] // HEAD_DIM == lane[None, :] // HEAD_DIM).astype(np.float32) / HEAD_DIM


def _overlap_t(n_cmp, n_cmp_pad, n_sb):
    c_start = np.arange(n_cmp)[:, None] * NSA_CMP_STRIDE
    js = np.arange(n_sb)[None, :]
    ov = ((c_start < (js + 1) * NSA_SEL_BLOCK) & (c_start + NSA_CMP_BLOCK > js * NSA_SEL_BLOCK))
    out = np.zeros((n_sb, n_cmp_pad), np.float32)
    out[:, :n_cmp] = ov.T
    return out


def _gate_expand_t(branch):
    out = np.zeros((NSA_WIDTH, LANES), np.float32)
    for slot, head in enumerate(NSA_HEAD_ORDER):
        out[slot * HEAD_DIM:(slot + 1) * HEAD_DIM, 3 * head + branch] = 1.0
    return out


def _block_diag2(w):
    z = jnp.zeros_like(w)
    return jnp.concatenate([jnp.concatenate([w, z], axis=1), jnp.concatenate([z, w], axis=1)], axis=0)


def _layer(x, c, positions, w_ada, b_ada, g_attn_norm, w_in, g_q_nsa, g_k_cmp, g_k_slc, g_k_win,
           cmp_k_pos, cmp_k_w1, cmp_k_w2, cmp_v_pos, cmp_v_w1, cmp_v_w2, g_q_moba, g_k_moba,
           w_branch_nsa, w_branch_moba, w_out, g_ffn_norm, w_ffn_up, conv_w, conv_b, w_ffn_down):
    b, s, d = x.shape
    order = np.asarray(NSA_HEAD_ORDER)
    off = IN_OFFSETS

    mod = _mod_call(c, w_ada, b_ada)
    mod4 = mod.reshape(b, 6, 1, d)

    col = lambda k: w_in[:, off[k]:off[k + 1]]
    w_qn = col(0).reshape(d, NSA_HEADS, HEAD_DIM)[:, order].reshape(d, NSA_WIDTH)
    w_gn = jnp.pad(col(7), ((0, 0), (0, LANES - 3 * NSA_HEADS)))
    w_p = jnp.concatenate([w_qn, col(8), col(9), col(10), col(1), col(2), col(3), col(4), col(5),
                           col(6), w_gn], axis=1).astype(BF16)
    tile2 = lambda gv: jnp.tile(gv, LANES // HEAD_DIM)
    gains = jnp.stack([tile2(g_q_nsa), tile2(g_q_moba), tile2(g_k_moba), tile2(g_k_slc),
                       tile2(g_k_win)] + [jnp.ones((LANES,), F32)] * 3)
    bd = jnp.asarray(_block_diag_mean(), BF16)
    half = HEAD_DIM // 2
    inv_freq = ROPE_THETA ** (-jnp.arange(half, dtype=F32) / half)
    inv_freq = jnp.tile(inv_freq, LANES // half).reshape(1, LANES)
    pos_packed = jnp.repeat(positions.reshape(b, s // ROPE_PACK, ROPE_PACK), ROPE_FREQS, axis=2)

    (qn, qm, km, vm, kc, vc, ksl, vsl, kwn, vwn, gn) = _inproj_call(
        x, mod4, g_attn_norm.reshape(1, d), pos_packed, inv_freq, w_p, gains, bd)

    n_grp = s // NSA_CMP_STRIDE
    n_cmp = (s - NSA_CMP_BLOCK) // NSA_CMP_STRIDE + 1
    w1_pack = lambda w1: jax.vmap(_block_diag2)(
        w1.reshape(NSA_CMP_BLOCK, HEAD_DIM, HEAD_DIM)).astype(BF16)
    pos_pack = lambda p: jnp.tile(p, (1, NSA_KV_HEADS))
    kcmp, vcmp = _compress_call(
        kc, vc, pos_pack(cmp_k_pos), pos_pack(cmp_v_pos),
        w1_pack(cmp_k_w1), w1_pack(cmp_v_w1), _block_diag2(cmp_k_w2).astype(BF16),
        _block_diag2(cmp_v_w2).astype(BF16), tile2(g_k_cmp).reshape(1, LANES), bd)

    n_sb = s // NSA_SEL_BLOCK
    ovt = jnp.asarray(_overlap_t(n_cmp, n_grp, n_sb), BF16)
    ocmpt, selt = _nsa_cmp_call(qn, gn, kcmp, vcmp, ovt, jnp.asarray(_gate_expand_t(0), BF16))
    oa = _nsa_attn_call(qn, selt, gn, ocmpt, ksl, vsl, kwn, vwn,
                        jnp.asarray(_gate_expand_t(1), BF16), jnp.asarray(_gate_expand_t(2), BF16))

    ob = _moba_call(qm, km, vm)

    w_bn = w_branch_nsa.reshape(NSA_HEADS, HEAD_DIM, d)[order].reshape(NSA_WIDTH, d)
    x1 = _merge_call(x, mod4, g_attn_norm.reshape(1, d), oa, ob, col(11).astype(BF16),
                     col(12).astype(BF16), w_bn.astype(BF16), w_branch_moba.astype(BF16),
                     w_out.astype(BF16))

    return _ffn_call(x1, mod4, g_ffn_norm.reshape(1, d), w_ffn_up.astype(BF16), conv_w,
                     conv_b.reshape(1, -1), w_ffn_down.astype(BF16))


def kernel(x, c, positions, w_ada, b_ada, g_attn_norm, w_in, g_q_nsa, g_k_cmp, g_k_slc, g_k_win,
           cmp_k_pos, cmp_k_w1, cmp_k_w2, cmp_v_pos, cmp_v_w1, cmp_v_w2, g_q_moba, g_k_moba,
           w_branch_nsa, w_branch_moba, w_out, g_ffn_norm, w_ffn_up, conv_w, conv_b, w_ffn_down):
    for l in range(w_ada.shape[0]):
        x = _layer(x, c, positions, w_ada[l], b_ada[l], g_attn_norm[l], w_in[l], g_q_nsa[l],
                   g_k_cmp[l], g_k_slc[l], g_k_win[l], cmp_k_pos[l], cmp_k_w1[l], cmp_k_w2[l],
                   cmp_v_pos[l], cmp_v_w1[l], cmp_v_w2[l], g_q_moba[l], g_k_moba[l],
                   w_branch_nsa[l], w_branch_moba[l], w_out[l], g_ffn_norm[l], w_ffn_up[l],
                   conv_w[l], conv_b[l], w_ffn_down[l])
    return x
```

```python
import numpy as np
import jax
import jax.numpy as jnp
from jax import lax
from jax.experimental import pallas as pl
from jax.experimental.pallas import tpu as pltpu

F32 = jnp.float32
BF16 = jnp.bfloat16

D_MODEL = 1024
HEAD_DIM = 64
NSA_HEADS = 8
NSA_KV_HEADS = 2
NSA_CMP_BLOCK = 32
NSA_CMP_STRIDE = 16
NSA_SEL_BLOCK = 64
NSA_SEL_TOPN = 16
NSA_WINDOW = 512
NSA_FORCE_BONUS = 1e4
MOBA_HEADS = 8
MOBA_BLOCK = 256
MOBA_TOPK = 3
CONV_WIDTH = 3
ROPE_THETA = 10000.0
NORM_EPS = 1e-6
NEG_INF = -1e30

LANES = 128
LOG2_E = 1.4426950408889634
SCALE = HEAD_DIM ** -0.5 * LOG2_E
MASK_BIAS = -(2.0 ** 100)
VMEM_LIMIT = 56 * 1024 * 1024

TM_PROJ = 512
TM_INPROJ = 1024
TQ = 256
TQ_CMP = 1024
FF_CHUNK = 256

NSA_WIDTH = NSA_HEADS * HEAD_DIM
MOBA_WIDTH = MOBA_HEADS * HEAD_DIM
KV_WIDTH = NSA_KV_HEADS * HEAD_DIM
IN_SIZES = (NSA_WIDTH, KV_WIDTH, KV_WIDTH, KV_WIDTH, KV_WIDTH, KV_WIDTH, KV_WIDTH,
            3 * NSA_HEADS, MOBA_WIDTH, MOBA_WIDTH, MOBA_WIDTH, D_MODEL, D_MODEL)
IN_OFFSETS = np.concatenate([[0], np.cumsum(IN_SIZES)]).tolist()

NSA_HEAD_ORDER = (0, 4, 1, 5, 2, 6, 3, 7)


def _dot(a, b):
    return jnp.dot(a, b, preferred_element_type=F32)


def _dot_nt(a, b):
    return lax.dot_general(a, b, (((1,), (1,)), ((), ())), preferred_element_type=F32)


def _split_bf16(v):
    hi = v.astype(BF16)
    lo = (v - hi.astype(F32)).astype(BF16)
    return hi, lo


def _cparams(*sem, flags=None):
    return pltpu.CompilerParams(dimension_semantics=sem, vmem_limit_bytes=VMEM_LIMIT, flags=flags)


def _const_spec(shape):
    n = len(shape)
    return pl.BlockSpec(shape, lambda *_: (0,) * n, pipeline_mode=pl.Buffered(1))


def _adaln_norm(x, g, sc, sh):
    y = x * lax.rsqrt(jnp.mean(x * x, axis=-1, keepdims=True) + NORM_EPS)
    return (y * g) * (1.0 + sc) + sh


def _mod_kernel(c_ref, w_ref, b_ref, o_ref):
    o_ref[...] = jnp.dot(c_ref[...], w_ref[...], preferred_element_type=F32,
                         precision=lax.Precision.HIGHEST) + b_ref[...]


def _mod_call(c, w_ada, b_ada):
    b, d = c.shape
    n = w_ada.shape[1]
    tn = D_MODEL
    return pl.pallas_call(
        _mod_kernel,
        grid=(n // tn,),
        in_specs=[pl.BlockSpec((b, d), lambda j: (0, 0)),
                  pl.BlockSpec((d, tn), lambda j: (0, j)),
                  pl.BlockSpec((1, tn), lambda j: (0, j))],
        out_specs=pl.BlockSpec((b, tn), lambda j: (0, j)),
        out_shape=jax.ShapeDtypeStruct((b, n), F32),
        compiler_params=_cparams("parallel"),
        name="mod",
    )(c, w_ada, b_ada.reshape(1, n))


_P_QN, _P_QM, _P_KM, _P_VM = 0, 512, 1024, 1536
_P_KC, _P_VC, _P_KSL, _P_VSL, _P_KWN, _P_VWN, _P_GN = 2048, 2176, 2304, 2432, 2560, 2688, 2816
_P_WIDTH = 2944
ROPE_FREQS = HEAD_DIM // 2
ROPE_PACK = LANES // ROPE_FREQS


def _inproj_kernel(x_ref, sc_ref, sh_ref, g_ref, pos_ref, invf_ref, w_ref, gains_ref, bd_ref,
                   qn_ref, qm_ref, km_ref, vm_ref, kc_ref, vc_ref, ksl_ref, vsl_ref, kwn_ref,
                   vwn_ref, gn_ref, cos_ref, sin_ref):
    x = x_ref[0]
    tm = x.shape[0]
    h = _adaln_norm(x, g_ref[...], sc_ref[...], sh_ref[...]).astype(BF16)

    ang = pos_ref[0].astype(F32) * invf_ref[...]
    cos_p = jnp.cos(ang)
    sin_p = jnp.sin(ang)
    group = lax.broadcasted_iota(jnp.int32, ang.shape, 1) // ROPE_FREQS
    for k in range(ROPE_PACK):
        for packed, dst in ((cos_p, cos_ref), (sin_p, sin_ref)):
            own = jnp.where(group == k, packed, 0.0)
            spread = own
            for shift in range(1, ROPE_PACK):
                spread = spread + pltpu.roll(own, shift * ROPE_FREQS, 1)
            dst[pl.ds(k, tm // ROPE_PACK, stride=ROPE_PACK), :] = spread
    cos = cos_ref[...]
    sin = sin_ref[...]
    lane = lax.broadcasted_iota(jnp.int32, (tm, LANES), 1)
    first = (lane & (HEAD_DIM // 2)) == 0
    sin_signed = jnp.where(first, -sin, sin)
    bd = bd_ref[...]

    def head_norm(y, gain):
        ms = _dot((y * y).astype(BF16), bd)
        return y * lax.rsqrt(ms + NORM_EPS) * gain

    def rope(y):
        partner = jnp.where(first, pltpu.roll(y, LANES - HEAD_DIM // 2, 1),
                            pltpu.roll(y, HEAD_DIM // 2, 1))
        return y * cos + partner * sin_signed

    def proj(off, width):
        return _dot(h, w_ref[:, off:off + width])

    def wide(off, out_ref, gain_row, scale):
        acc = proj(off, 4 * LANES)
        gain = gains_ref[gain_row:gain_row + 1, :]
        for p in range(4):
            y = rope(head_norm(acc[:, p * LANES:(p + 1) * LANES], gain))
            if scale != 1.0:
                y = y * scale
            out_ref[0, :, p * LANES:(p + 1) * LANES] = y.astype(BF16)

    wide(_P_QN, qn_ref, 0, SCALE)
    wide(_P_QM, qm_ref, 1, SCALE)
    wide(_P_KM, km_ref, 2, 1.0)
    vm_ref[0] = proj(_P_VM, 4 * LANES).astype(BF16)

    small = proj(_P_KC, 7 * LANES)
    kc_ref[0] = rope(small[:, 0:LANES])
    vc_ref[0] = small[:, LANES:2 * LANES]
    ksl_ref[0] = rope(head_norm(small[:, 2 * LANES:3 * LANES], gains_ref[3:4, :])).astype(BF16)
    vsl_ref[0] = small[:, 3 * LANES:4 * LANES].astype(BF16)
    kwn_ref[0] = rope(head_norm(small[:, 4 * LANES:5 * LANES], gains_ref[4:5, :])).astype(BF16)
    vwn_ref[0] = small[:, 5 * LANES:6 * LANES].astype(BF16)
    gn_ref[0] = jax.nn.sigmoid(small[:, 6 * LANES:7 * LANES])


def _inproj_call(x, mod4, g_attn, pos_packed, inv_freq, w_p, gains, bd):
    b, s, d = x.shape
    tm = min(TM_INPROJ, s)
    tok = lambda width: pl.BlockSpec((1, tm, width), lambda bi, i: (bi, i, 0))
    modspec = lambda k: pl.BlockSpec((None, None, 1, d), lambda bi, i: (bi, k, 0, 0))
    shapes = [jax.ShapeDtypeStruct((b, s, 4 * LANES), BF16)] * 4 \
        + [jax.ShapeDtypeStruct((b, s, LANES), F32)] * 2 \
        + [jax.ShapeDtypeStruct((b, s, LANES), BF16)] * 4 \
        + [jax.ShapeDtypeStruct((b, s, LANES), F32)]
    return pl.pallas_call(
        _inproj_kernel,
        grid=(b, s // tm),
        in_specs=[tok(d), modspec(1), modspec(0), _const_spec((1, d)),
                  pl.BlockSpec((1, tm // ROPE_PACK, LANES), lambda bi, i: (bi, i, 0)),
                  _const_spec((1, LANES)), _const_spec((d, _P_WIDTH)),
                  _const_spec((8, LANES)), _const_spec((LANES, LANES))],
        out_specs=[tok(4 * LANES)] * 4 + [tok(LANES)] * 7,
        out_shape=shapes,
        scratch_shapes=[pltpu.VMEM((tm, LANES), F32)] * 2,
        compiler_params=_cparams("parallel", "parallel"),
        name="inproj",
    )(x, mod4, mod4, g_attn, pos_packed, inv_freq, w_p, gains, bd)


def _compress_kernel(xk_ref, xv_ref, posk_ref, posv_ref, w1k_ref, w1v_ref, w2k_ref, w2v_ref,
                     gain_ref, bd_ref, ko_ref, vo_ref):
    n_grp = ko_ref.shape[1]

    def mlp(x_ref, pos_ref, w1_ref, w2_ref):
        first = None
        second = None
        for l in range(NSA_CMP_STRIDE):
            xl = x_ref[0, pl.ds(l, n_grp, stride=NSA_CMP_STRIDE), :]
            a = _dot((xl + pos_ref[l:l + 1, :]).astype(BF16), w1_ref[l])
            lb = NSA_CMP_STRIDE + l
            b2 = _dot((xl + pos_ref[lb:lb + 1, :]).astype(BF16), w1_ref[lb])
            first = a if first is None else first + a
            second = b2 if second is None else second + b2
        hidden = first + pltpu.roll(second, n_grp - 1, 0)
        return _dot(jax.nn.gelu(hidden).astype(BF16), w2_ref[...])

    k = mlp(xk_ref, posk_ref, w1k_ref, w2k_ref)
    ms = _dot((k * k).astype(BF16), bd_ref[...])
    ko_ref[0] = (k * lax.rsqrt(ms + NORM_EPS) * gain_ref[...]).astype(BF16)
    vo_ref[0] = mlp(xv_ref, posv_ref, w1v_ref, w2v_ref).astype(BF16)


def _compress_call(xk, xv, posk, posv, w1k, w1v, w2k, w2v, gain, bd):
    b, s, w = xk.shape
    n = s // NSA_CMP_STRIDE
    xspec = pl.BlockSpec((1, s, w), lambda bi: (bi, 0, 0))
    ospec = pl.BlockSpec((1, n, LANES), lambda bi: (bi, 0, 0))
    return pl.pallas_call(
        _compress_kernel,
        grid=(b,),
        in_specs=[xspec, xspec, _const_spec(posk.shape), _const_spec(posv.shape),
                  _const_spec(w1k.shape), _const_spec(w1v.shape), _const_spec(w2k.shape),
                  _const_spec(w2v.shape), _const_spec((1, LANES)), _const_spec((LANES, LANES))],
        out_specs=[ospec, ospec],
        out_shape=[jax.ShapeDtypeStruct((b, n, LANES), BF16)] * 2,
        compiler_params=_cparams("parallel"),
        name="compress",
    )(xk, xv, posk, posv, w1k, w1v, w2k, w2v, gain, bd)


def _lane_half0(tq):
    return lax.broadcasted_iota(jnp.int32, (tq, LANES), 1) < HEAD_DIM


def _dim_half0(tq):
    return lax.broadcasted_iota(jnp.int32, (LANES, tq), 0) < HEAD_DIM


def _take_half(q2, half0, g):
    qf = q2.astype(F32)
    keep = half0 if g == 0 else jnp.logical_not(half0)
    return jnp.where(keep, qf, 0.0).astype(BF16)


def _transposed_bf16(rows):
    return rows.astype(F32).T.astype(BF16)


def _rank_counts(vals, jb, n):
    counts = jnp.zeros_like(vals)
    for j in range(n):
        row = vals[j:j + 1, :]
        beats = jnp.where(vals > row, 1.0,
                          jnp.where(vals == row, jnp.where(jb < j, 1.0, 0.0), 0.0))
        cnt = jnp.sum(beats, axis=0, keepdims=True)
        counts = jnp.where(jb == j, cnt, counts)
    return counts


def _values_t_with_ones(rows, dhalf0):
    vt = rows.astype(F32).T
    return (jnp.where(dhalf0, vt, 1.0).astype(BF16), jnp.where(dhalf0, 1.0, vt).astype(BF16))


def _softmax_max(parts):
    m = None
    for s, _, bias in parts:
        if bias is None:
            mx = jnp.max(s, axis=0, keepdims=True)
        else:
            n_blk = bias.shape[0]
            blk = s.shape[0] // n_blk
            mx = None
            for b in range(n_blk):
                mb = jnp.max(s[b * blk:(b + 1) * blk], axis=0, keepdims=True) + bias[b:b + 1]
                mx = mb if mx is None else jnp.maximum(mx, mb)
        m = mx if m is None else jnp.maximum(m, mx)
    return m


def _softmax_values(parts, m):
    acc = None
    for s, vt, bias in parts:
        if bias is None:
            p = jnp.exp2(s - m)
        else:
            n_blk = bias.shape[0]
            blk = s.shape[0] // n_blk
            p = jnp.concatenate([jnp.exp2(s[b * blk:(b + 1) * blk] - (m - bias[b:b + 1]))
                                 for b in range(n_blk)], axis=0)
        pv = _dot(vt, p.astype(BF16))
        acc = pv if acc is None else acc + pv
    return acc


def _solve_all(problems):
    all_parts = [problem() for problem in problems]
    maxima = [_softmax_max(parts) for parts in all_parts]
    return [_softmax_values(parts, m) for parts, m in zip(all_parts, maxima)]


def _normalized_pair(acc0, acc1, dhalf0):
    l0 = jnp.maximum(acc0[HEAD_DIM:HEAD_DIM + 1, :], 1e-30)
    l1 = jnp.maximum(acc1[0:1, :], 1e-30)
    return jnp.where(dhalf0, acc0 * (1.0 / l0), acc1 * (1.0 / l1))


NSA_TILES_PER_STEP = 4
MOBA_TILES_PER_STEP = 8


def _tile_groups(n_tiles, tiles_per_step):
    pairs = [[j + 1] if n_tiles - j == j + 1 else [j + 1, n_tiles - j]
             for j in range((n_tiles + 1) // 2)]
    per_group = max(tiles_per_step // 2, 1)
    return [sum(pairs[i:i + per_group], []) for i in range(0, len(pairs), per_group)]


def _grouped_tile_cases(step, groups, tiles_case):
    for j, tiles in enumerate(groups):
        pl.when(step == j)(lambda tiles=tiles: tiles_case(tiles))


def _nsa_cmp_kernel(q_ref, gate_ref, kc_ref, vc_ref, ovt_ref, egt_ref, ocmpt_ref, selt_ref):
    tq = q_ref.shape[1]
    n_c = kc_ref.shape[1]
    t0 = pl.program_id(1) * tq
    g_hi, g_lo = _split_bf16(gate_ref[0])

    c_row = lax.broadcasted_iota(jnp.int32, (n_c, tq), 0)
    t_col = t0 + lax.broadcasted_iota(jnp.int32, (n_c, tq), 1)
    visible = c_row * NSA_CMP_STRIDE + (NSA_CMP_BLOCK - 1) <= t_col
    half0 = _lane_half0(tq)
    dhalf0 = _dim_half0(tq)
    kc = kc_ref[0]
    vct = _transposed_bf16(vc_ref[0])

    n_pairs = NSA_HEADS // NSA_KV_HEADS
    scores = [[_dot_nt(kc, _take_half(q_ref[0, :, r * LANES:(r + 1) * LANES], half0, g))
               for g in range(NSA_KV_HEADS)] for r in range(n_pairs)]
    psum = [None, None]
    for r in range(n_pairs):
        o_pair = None
        for g in range(NSA_KV_HEADS):
            s = jnp.where(visible, scores[r][g], NEG_INF)
            m = jnp.max(s, axis=0, keepdims=True)
            p = jnp.where(visible, jnp.exp2(s - m), 0.0)
            p = p / jnp.maximum(jnp.sum(p, axis=0, keepdims=True), 1e-30)
            psum[g] = p if psum[g] is None else psum[g] + p
            o = _dot(vct, p.astype(BF16))
            o_pair = o if g == 0 else jnp.where(dhalf0, o_pair, o)
        egt = egt_ref[r * LANES:(r + 1) * LANES, :]
        gate_t = _dot_nt(egt, g_hi) + _dot_nt(egt, g_lo)
        ocmpt_ref[0, r] = gate_t * o_pair

    n_sb = ovt_ref.shape[0]
    n_top = min(NSA_SEL_TOPN, n_sb)
    few_candidates = (t0 + tq) // NSA_SEL_BLOCK <= n_top

    @pl.when(few_candidates)
    def _():
        selt_ref[...] = jnp.zeros_like(selt_ref)

    @pl.when(jnp.logical_not(few_candidates))
    def _():
        jb = lax.broadcasted_iota(jnp.int32, (n_sb, tq), 0)
        own = (t0 + lax.broadcasted_iota(jnp.int32, (n_sb, tq), 1)) // NSA_SEL_BLOCK
        forced = (jb == 0) | (jb == own) | (jb == own - 1)
        ovt = ovt_ref[...]
        for g in range(NSA_KV_HEADS):
            p_hi, p_lo = _split_bf16(psum[g])
            imp = _dot(ovt, p_hi) + _dot(ovt, p_lo)
            imp = jnp.where(jb <= own, imp + jnp.where(forced, NSA_FORCE_BONUS, 0.0), NEG_INF)
            counts = _rank_counts(imp, jb, n_sb)
            selt_ref[0, g] = jnp.where(counts < float(n_top), 0.0, MASK_BIAS)


def _nsa_cmp_call(qn, gn, kcmp, vcmp, ovt, eg_cmp_t):
    b, s, w = qn.shape
    tq = min(TQ_CMP, s)
    n_pairs = w // LANES
    n_sb = ovt.shape[0]
    tok = lambda width: pl.BlockSpec((1, tq, width), lambda bi, i: (bi, i, 0))
    cspec = pl.BlockSpec((1, kcmp.shape[1], LANES), lambda bi, i: (bi, 0, 0))
    return pl.pallas_call(
        _nsa_cmp_kernel,
        grid=(b, s // tq),
        in_specs=[tok(w), tok(LANES), cspec, cspec, _const_spec(ovt.shape),
                  _const_spec(eg_cmp_t.shape)],
        out_specs=[pl.BlockSpec((1, n_pairs, LANES, tq), lambda bi, i: (bi, 0, 0, i)),
                   pl.BlockSpec((1, NSA_KV_HEADS, n_sb, tq), lambda bi, i: (bi, 0, 0, i))],
        out_shape=[jax.ShapeDtypeStruct((b, n_pairs, LANES, s), F32),
                   jax.ShapeDtypeStruct((b, NSA_KV_HEADS, n_sb, s), F32)],
        compiler_params=_cparams("parallel", "parallel"),
        name="nsa_cmp",
    )(qn, gn, kcmp, vcmp, ovt, eg_cmp_t)


def _nsa_attn_kernel(q_ref, selt_ref, gate_ref, ocmpt_ref, ksl_ref, vsl_ref, kwn_ref, vwn_ref,
                     egst_ref, egwt_ref, o_ref, vslt_ref, vwnt_ref):
    s_len = q_ref.shape[1]
    tq = min(TQ, s_len)
    n_tiles = s_len // tq
    step = pl.program_id(2)
    half0 = _lane_half0(tq)
    dhalf0 = _dim_half0(tq)
    k_row = lax.broadcasted_iota(jnp.int32, (tq, tq), 0)
    q_col = lax.broadcasted_iota(jnp.int32, (tq, tq), 1)
    causal = k_row <= q_col
    blocks_per_chunk = tq // NSA_SEL_BLOCK

    @pl.when(jnp.logical_and(pl.program_id(1) == 0, step == 0))
    def _():
        for c in range(n_tiles):
            cols = slice(c * tq, (c + 1) * tq)
            vslt_ref[0, :, cols], vslt_ref[1, :, cols] = _values_t_with_ones(vsl_ref[0, cols, :],
                                                                             dhalf0)
            vwnt_ref[0, :, cols], vwnt_ref[1, :, cols] = _values_t_with_ones(vwn_ref[0, cols, :],
                                                                             dhalf0)

    def sel_problem(n, qh, g):
        lo = (n - 1) * tq
        all_selected = n * blocks_per_chunk <= NSA_SEL_TOPN
        parts = []
        for c in range(n):
            k0 = c * tq
            sc = _dot_nt(ksl_ref[0, k0:k0 + tq, :], qh)
            if c == n - 1:
                sc = jnp.where(causal, sc, NEG_INF)
            bias = None
            if not all_selected:
                j0 = c * blocks_per_chunk
                bias = selt_ref[0, g, j0:j0 + blocks_per_chunk, lo:lo + tq]
            parts.append((sc, vslt_ref[g, :, k0:k0 + tq], bias))
        return parts

    def win_problem(n, qh, g):
        lo = (n - 1) * tq
        parts = [(jnp.where(causal, _dot_nt(kwn_ref[0, lo:lo + tq, :], qh), NEG_INF),
                  vwnt_ref[g, :, lo:lo + tq], None)]
        for back in range(1, min(n - 1, (NSA_WINDOW + tq - 2) // tq) + 1):
            k0 = lo - back * tq
            sw = _dot_nt(kwn_ref[0, k0:k0 + tq, :], qh)
            if (back + 1) * tq - 1 >= NSA_WINDOW:
                sw = jnp.where(q_col - k_row + back * tq < NSA_WINDOW, sw, NEG_INF)
            parts.append((sw, vwnt_ref[g, :, k0:k0 + tq], None))
        return parts

    def tiles_case(tiles):
        problems = []
        for n in tiles:
            q2 = q_ref[0, (n - 1) * tq:n * tq, :]
            for g in range(NSA_KV_HEADS):
                qh = _take_half(q2, half0, g)
                problems.append(lambda n=n, qh=qh, g=g: sel_problem(n, qh, g))
                problems.append(lambda n=n, qh=qh, g=g: win_problem(n, qh, g))
        accs = _solve_all(problems)
        for i, n in enumerate(tiles):
            lo = (n - 1) * tq
            sel0, win0, sel1, win1 = accs[4 * i:4 * i + 4]
            g_hi, g_lo = _split_bf16(gate_ref[0, lo:lo + tq, :])
            gs_t = _dot_nt(egst_ref[...], g_hi) + _dot_nt(egst_ref[...], g_lo)
            gw_t = _dot_nt(egwt_ref[...], g_hi) + _dot_nt(egwt_ref[...], g_lo)
            out_t = (ocmpt_ref[0, 0, :, lo:lo + tq] + gs_t * _normalized_pair(sel0, sel1, dhalf0)
                     + gw_t * _normalized_pair(win0, win1, dhalf0))
            o_ref[0, lo:lo + tq, :] = out_t.T.astype(BF16)

    _grouped_tile_cases(step, _tile_groups(n_tiles, NSA_TILES_PER_STEP), tiles_case)


def _nsa_attn_call(qn, selt, gn, ocmpt, ksl, vsl, kwn, vwn, eg_sel_t, eg_win_t):
    b, s, _ = qn.shape
    n_tiles = s // min(TQ, s)
    n_pairs = qn.shape[2] // LANES
    pair = pl.BlockSpec((1, s, LANES), lambda bi, r, j: (bi, 0, r))
    seq = pl.BlockSpec((1, s, LANES), lambda bi, r, j: (bi, 0, 0))
    egspec = pl.BlockSpec((LANES, LANES), lambda bi, r, j: (r, 0))
    return pl.pallas_call(
        _nsa_attn_kernel,
        grid=(b, n_pairs, len(_tile_groups(n_tiles, NSA_TILES_PER_STEP))),
        in_specs=[pair,
                  pl.BlockSpec((1,) + selt.shape[1:], lambda bi, r, j: (bi, 0, 0, 0)),
                  seq,
                  pl.BlockSpec((1, 1, LANES, s), lambda bi, r, j: (bi, r, 0, 0)),
                  seq, seq, seq, seq, egspec, egspec],
        out_specs=pair,
        out_shape=jax.ShapeDtypeStruct(qn.shape, BF16),
        scratch_shapes=[pltpu.VMEM((NSA_KV_HEADS, LANES, s), BF16)] * 2,
        compiler_params=_cparams("parallel", "arbitrary", "arbitrary"),
        name="nsa_attn",
    )(qn, selt, gn, ocmpt, ksl, vsl, kwn, vwn, eg_sel_t, eg_win_t)


def _moba_kernel(q_ref, k_ref, v_ref, o_ref, vt_ref, kmean_ref):
    s_len = q_ref.shape[1]
    tq = min(MOBA_BLOCK, s_len)
    nb = s_len // tq
    n_top = min(MOBA_TOPK, nb - 1)
    step = pl.program_id(2)
    half0 = _lane_half0(tq)
    dhalf0 = _dim_half0(tq)
    k_row = lax.broadcasted_iota(jnp.int32, (tq, tq), 0)
    q_col = lax.broadcasted_iota(jnp.int32, (tq, tq), 1)
    causal = k_row <= q_col
    jb = lax.broadcasted_iota(jnp.int32, (nb, tq), 0)

    @pl.when(step == 0)
    def _():
        for j in range(nb):
            rows = slice(j * tq, (j + 1) * tq)
            kmean_ref[j:j + 1, :] = jnp.mean(k_ref[0, rows, :].astype(F32), axis=0, keepdims=True)
            vt_ref[0, :, rows], vt_ref[1, :, rows] = _values_t_with_ones(v_ref[0, rows, :], dhalf0)

    def problem(n, qh, h):
        lo = (n - 1) * tq
        n_past = n - 1
        bias = None
        if n_past > n_top:
            km_hi, km_lo = _split_bf16(kmean_ref[...])
            gate = _dot_nt(km_hi, qh) + _dot_nt(km_lo, qh)
            gate = jnp.where(jb < n_past, gate, NEG_INF)
            counts = _rank_counts(gate, jb, n_past)
            bias = jnp.where(counts < float(n_top), 0.0, MASK_BIAS)
        parts = []
        for j in range(n_past):
            sj = _dot_nt(k_ref[0, j * tq:(j + 1) * tq, :], qh)
            parts.append((sj, vt_ref[h, :, j * tq:(j + 1) * tq],
                          None if bias is None else bias[j:j + 1, :]))
        s_own = _dot_nt(k_ref[0, lo:lo + tq, :], qh)
        parts.append((jnp.where(causal, s_own, NEG_INF), vt_ref[h, :, lo:lo + tq], None))
        return parts

    def tiles_case(tiles):
        problems = []
        for n in tiles:
            q2 = q_ref[0, (n - 1) * tq:n * tq, :]
            for h in range(2):
                qh = _take_half(q2, half0, h)
                problems.append(lambda n=n, qh=qh, h=h: problem(n, qh, h))
        accs = _solve_all(problems)
        for i, n in enumerate(tiles):
            out_t = _normalized_pair(accs[2 * i], accs[2 * i + 1], dhalf0)
            o_ref[0, (n - 1) * tq:n * tq, :] = out_t.T.astype(BF16)

    _grouped_tile_cases(step, _tile_groups(nb, MOBA_TILES_PER_STEP), tiles_case)


def _moba_call(qm, km, vm):
    b, s, w = qm.shape
    nb = s // min(MOBA_BLOCK, s)
    n_pairs = w // LANES
    seq = pl.BlockSpec((1, s, LANES), lambda bi, p, j: (bi, 0, p))
    return pl.pallas_call(
        _moba_kernel,
        grid=(b, n_pairs, len(_tile_groups(nb, MOBA_TILES_PER_STEP))),
        in_specs=[seq, seq, seq],
        out_specs=seq,
        out_shape=jax.ShapeDtypeStruct(qm.shape, BF16),
        scratch_shapes=[pltpu.VMEM((2, LANES, s), BF16), pltpu.VMEM((nb, LANES), F32)],
        compiler_params=_cparams("parallel", "parallel", "arbitrary"),
        name="moba",
    )(qm, km, vm)


def _merge_kernel(x_ref, sc_ref, sh_ref, gt_ref, g_ref, oa_ref, ob_ref, wga_ref, wgb_ref,
                  wbn_ref, wbm_ref, wo_ref, o_ref):
    x = x_ref[0]
    h = _adaln_norm(x, g_ref[...], sc_ref[...], sh_ref[...]).astype(BF16)
    ga = jax.nn.sigmoid(_dot(h, wga_ref[...]))
    gb = jax.nn.sigmoid(_dot(h, wgb_ref[...]))
    mixed = ga * _dot(oa_ref[0], wbn_ref[...]) + gb * _dot(ob_ref[0], wbm_ref[...])
    o_ref[0] = x + gt_ref[...] * _dot(mixed.astype(BF16), wo_ref[...])


def _merge_call(x, mod4, g_attn, oa, ob, wga, wgb, wbn, wbm, wo):
    b, s, d = x.shape
    tm = min(TM_PROJ, s)
    tok = lambda width: pl.BlockSpec((1, tm, width), lambda bi, i: (bi, i, 0))
    modspec = lambda k: pl.BlockSpec((None, None, 1, d), lambda bi, i: (bi, k, 0, 0))
    return pl.pallas_call(
        _merge_kernel,
        grid=(b, s // tm),
        in_specs=[tok(d), modspec(1), modspec(0), modspec(2), _const_spec((1, d)),
                  tok(oa.shape[2]), tok(ob.shape[2]), _const_spec(wga.shape),
                  _const_spec(wgb.shape), _const_spec(wbn.shape), _const_spec(wbm.shape),
                  _const_spec(wo.shape)],
        out_specs=tok(d),
        out_shape=jax.ShapeDtypeStruct(x.shape, F32),
        compiler_params=_cparams("parallel", "parallel"),
        name="merge",
    )(x, mod4, mod4, mod4, g_attn, oa, ob, wga, wgb, wbn, wbm, wo)


HALO = 8


def _ffn_kernel(x_ref, halo_ref, sc_ref, sh_ref, gt_ref, g_ref, wup_ref, cw_ref, cb_ref,
                wd_ref, o_ref):
    x = x_ref[0]
    tm = x.shape[0]
    g = g_ref[...]
    h = _adaln_norm(x, g, sc_ref[...], sh_ref[...]).astype(BF16)
    h_halo = _adaln_norm(halo_ref[0], g, sc_ref[...], sh_ref[...]).astype(BF16)
    h_ext = jnp.concatenate([h_halo, h], axis=0)
    ext_row = lax.broadcasted_iota(jnp.int32, (HALO + tm, FF_CHUNK), 0)
    live = ext_row >= jnp.where(pl.program_id(1) > 0, 0, HALO)

    d_ff = wd_ref.shape[0]
    n_chunks = d_ff // FF_CHUNK
    cols = lambda c: slice(c * FF_CHUNK, (c + 1) * FF_CHUNK)

    def up(c):
        a = _dot(h_ext, wup_ref[:, cols(c)])
        v = _dot(h, wup_ref[:, d_ff + c * FF_CHUNK:d_ff + (c + 1) * FF_CHUNK])
        return jnp.where(live, a, 0.0), v

    acc = None
    ups = [up(c) for c in range(n_chunks)]
    for c in range(n_chunks):
        a, v = ups[c]
        cw = cw_ref[:, cols(c)]
        y = cb_ref[:, cols(c)] + cw[CONV_WIDTH - 1:CONV_WIDTH, :] * a[HALO:, :]
        for back in range(1, CONV_WIDTH):
            k = CONV_WIDTH - 1 - back
            y = y + cw[k:k + 1, :] * pltpu.roll(a, back, 0)[HALO:, :]
        gated = jax.nn.gelu(y) * v
        part = _dot(gated.astype(BF16), wd_ref[cols(c), :])
        acc = part if acc is None else acc + part
    o_ref[0] = x + gt_ref[...] * acc


def _ffn_call(x, mod4, g_ffn, wup, cw, cb, wd):
    b, s, d = x.shape
    tm = min(TM_PROJ, s)
    tok = pl.BlockSpec((1, tm, d), lambda bi, i: (bi, i, 0))
    halo = pl.BlockSpec((1, HALO, d), lambda bi, i: (bi, jnp.maximum(i * (tm // HALO) - 1, 0), 0))
    modspec = lambda k: pl.BlockSpec((None, None, 1, d), lambda bi, i: (bi, k, 0, 0))
    return pl.pallas_call(
        _ffn_kernel,
        grid=(b, s // tm),
        in_specs=[tok, halo, modspec(4), modspec(3), modspec(5), _const_spec((1, d)),
                  _const_spec(wup.shape), _const_spec(cw.shape), _const_spec(cb.shape),
                  _const_spec(wd.shape)],
        out_specs=tok,
        out_shape=jax.ShapeDtypeStruct(x.shape, F32),
        compiler_params=_cparams("parallel", "parallel"),
        name="ffn",
    )(x, x, mod4, mod4, mod4, g_ffn, wup, cw, cb, wd)


def _block_diag_mean():
    lane = np.arange(LANES)
    return (lane[:, None] // HEAD_DIM == lane[None, :] // HEAD_DIM).astype(np.float32) / HEAD_DIM


def _overlap_t(n_cmp, n_cmp_pad, n_sb):
    c_start = np.arange(n_cmp)[:, None] * NSA_CMP_STRIDE
    js = np.arange(n_sb)[None, :]
    ov = ((c_start < (js + 1) * NSA_SEL_BLOCK) & (c_start + NSA_CMP_BLOCK > js * NSA_SEL_BLOCK))
    out = np.zeros((n_sb, n_cmp_pad), np.float32)
    out[:, :n_cmp] = ov.T
    return out


def _gate_expand_t(branch):
    out = np.zeros((NSA_WIDTH, LANES), np.float32)
    for slot, head in enumerate(NSA_HEAD_ORDER):
        out[slot * HEAD_DIM:(slot + 1) * HEAD_DIM, 3 * head + branch] = 1.0
    return out


def _block_diag2(w):
    z = jnp.zeros_like(w)
    return jnp.concatenate([jnp.concatenate([w, z], axis=1), jnp.concatenate([z, w], axis=1)], axis=0)


def _layer(x, c, positions, w_ada, b_ada, g_attn_norm, w_in, g_q_nsa, g_k_cmp, g_k_slc, g_k_win,
           cmp_k_pos, cmp_k_w1, cmp_k_w2, cmp_v_pos, cmp_v_w1, cmp_v_w2, g_q_moba, g_k_moba,
           w_branch_nsa, w_branch_moba, w_out, g_ffn_norm, w_ffn_up, conv_w, conv_b, w_ffn_down):
    b, s, d = x.shape
    order = np.asarray(NSA_HEAD_ORDER)
    off = IN_OFFSETS

    mod = _mod_call(c, w_ada, b_ada)
    mod4 = mod.reshape(b, 6, 1, d)

    col = lambda k: w_in[:, off[k]:off[k + 1]]
    w_qn = col(0).reshape(d, NSA_HEADS, HEAD_DIM)[:, order].reshape(d, NSA_WIDTH)
    w_gn = jnp.pad(col(7), ((0, 0), (0, LANES - 3 * NSA_HEADS)))
    w_p = jnp.concatenate([w_qn, col(8), col(9), col(10), col(1), col(2), col(3), col(4), col(5),
                           col(6), w_gn], axis=1).astype(BF16)
    tile2 = lambda gv: jnp.tile(gv, LANES // HEAD_DIM)
    gains = jnp.stack([tile2(g_q_nsa), tile2(g_q_moba), tile2(g_k_moba), tile2(g_k_slc),
                       tile2(g_k_win)] + [jnp.ones((LANES,), F32)] * 3)
    bd = jnp.asarray(_block_diag_mean(), BF16)
    half = HEAD_DIM // 2
    inv_freq = ROPE_THETA ** (-jnp.arange(half, dtype=F32) / half)
    inv_freq = jnp.tile(inv_freq, LANES // half).reshape(1, LANES)
    pos_packed = jnp.repeat(positions.reshape(b, s // ROPE_PACK, ROPE_PACK), ROPE_FREQS, axis=2)

    (qn, qm, km, vm, kc, vc, ksl, vsl, kwn, vwn, gn) = _inproj_call(
        x, mod4, g_attn_norm.reshape(1, d), pos_packed, inv_freq, w_p, gains, bd)

    n_grp = s // NSA_CMP_STRIDE
    n_cmp = (s - NSA_CMP_BLOCK) // NSA_CMP_STRIDE + 1
    w1_pack = lambda w1: jax.vmap(_block_diag2)(
        w1.reshape(NSA_CMP_BLOCK, HEAD_DIM, HEAD_DIM)).astype(BF16)
    pos_pack = lambda p: jnp.tile(p, (1, NSA_KV_HEADS))
    kcmp, vcmp = _compress_call(
        kc, vc, pos_pack(cmp_k_pos), pos_pack(cmp_v_pos),
        w1_pack(cmp_k_w1), w1_pack(cmp_v_w1), _block_diag2(cmp_k_w2).astype(BF16),
        _block_diag2(cmp_v_w2).astype(BF16), tile2(g_k_cmp).reshape(1, LANES), bd)

    n_sb = s // NSA_SEL_BLOCK
    ovt = jnp.asarray(_overlap_t(n_cmp, n_grp, n_sb), BF16)
    ocmpt, selt = _nsa_cmp_call(qn, gn, kcmp, vcmp, ovt, jnp.asarray(_gate_expand_t(0), BF16))
    oa = _nsa_attn_call(qn, selt, gn, ocmpt, ksl, vsl, kwn, vwn,
                        jnp.asarray(_gate_expand_t(1), BF16), jnp.asarray(_gate_expand_t(2), BF16))

    ob = _moba_call(qm, km, vm)

    w_bn = w_branch_nsa.reshape(NSA_HEADS, HEAD_DIM, d)[order].reshape(NSA_WIDTH, d)
    x1 = _merge_call(x, mod4, g_attn_norm.reshape(1, d), oa, ob, col(11).astype(BF16),
                     col(12).astype(BF16), w_bn.astype(BF16), w_branch_moba.astype(BF16),
                     w_out.astype(BF16))

    return _ffn_call(x1, mod4, g_ffn_norm.reshape(1, d), w_ffn_up.astype(BF16), conv_w,
                     conv_b.reshape(1, -1), w_ffn_down.astype(BF16))


def kernel(x, c, positions, w_ada, b_ada, g_attn_norm, w_in, g_q_nsa, g_k_cmp, g_k_slc, g_k_win,
           cmp_k_pos, cmp_k_w1, cmp_k_w2, cmp_v_pos, cmp_v_w1, cmp_v_w2, g_q_moba, g_k_moba,
           w_branch_nsa, w_branch_moba, w_out, g_ffn_norm, w_ffn_up, conv_w, conv_b, w_ffn_down):
    for l in range(w_ada.shape[0]):
        x = _layer(x, c, positions, w_ada[l], b_ada[l], g_attn_norm[l], w_in[l], g_q_nsa[l],
                   g_k_cmp[l], g_k_slc[l], g_k_win[l], cmp_k_pos[l], cmp_k_w1[l], cmp_k_w2[l],
                   cmp_v_pos[l], cmp_v_w1[l], cmp_v_w2[l], g_q_moba[l], g_k_moba[l],
                   w_branch_nsa[l], w_branch_moba[l], w_out[l], g_ffn_norm[l], w_ffn_up[l],
                   conv_w[l], conv_b[l], w_ffn_down[l])
    return x
```

```python
import numpy as np
import jax
import jax.numpy as jnp
from jax import lax
from jax.experimental import pallas as pl
from jax.experimental.pallas import tpu as pltpu

F32 = jnp.float32
BF16 = jnp.bfloat16

D_MODEL = 1024
HEAD_DIM = 64
NSA_HEADS = 8
NSA_KV_HEADS = 2
NSA_CMP_BLOCK = 32
NSA_CMP_STRIDE = 16
NSA_SEL_BLOCK = 64
NSA_SEL_TOPN = 16
NSA_WINDOW = 512
NSA_FORCE_BONUS = 1e4
MOBA_HEADS = 8
MOBA_BLOCK = 256
MOBA_TOPK = 3
CONV_WIDTH = 3
ROPE_THETA = 10000.0
NORM_EPS = 1e-6
NEG_INF = -1e30

LANES = 128
LOG2_E = 1.4426950408889634
SCALE = HEAD_DIM ** -0.5 * LOG2_E
MASK_BIAS = -(2.0 ** 100)
VMEM_LIMIT = 56 * 1024 * 1024

TM_PROJ = 512
TM_INPROJ = 1024
TQ = 256
TQ_CMP = 1024
FF_CHUNK = 256

NSA_WIDTH = NSA_HEADS * HEAD_DIM
MOBA_WIDTH = MOBA_HEADS * HEAD_DIM
KV_WIDTH = NSA_KV_HEADS * HEAD_DIM
IN_SIZES = (NSA_WIDTH, KV_WIDTH, KV_WIDTH, KV_WIDTH, KV_WIDTH, KV_WIDTH, KV_WIDTH,
            3 * NSA_HEADS, MOBA_WIDTH, MOBA_WIDTH, MOBA_WIDTH, D_MODEL, D_MODEL)
IN_OFFSETS = np.concatenate([[0], np.cumsum(IN_SIZES)]).tolist()

NSA_HEAD_ORDER = (0, 4, 1, 5, 2, 6, 3, 7)


def _dot(a, b):
    return jnp.dot(a, b, preferred_element_type=F32)


def _dot_nt(a, b):
    return lax.dot_general(a, b, (((1,), (1,)), ((), ())), preferred_element_type=F32)


def _split_bf16(v):
    hi = v.astype(BF16)
    lo = (v - hi.astype(F32)).astype(BF16)
    return hi, lo


def _cparams(*sem, flags=None):
    return pltpu.CompilerParams(dimension_semantics=sem, vmem_limit_bytes=VMEM_LIMIT, flags=flags)


def _const_spec(shape):
    n = len(shape)
    return pl.BlockSpec(shape, lambda *_: (0,) * n, pipeline_mode=pl.Buffered(1))


def _adaln_norm(x, g, sc, sh):
    y = x * lax.rsqrt(jnp.mean(x * x, axis=-1, keepdims=True) + NORM_EPS)
    return (y * g) * (1.0 + sc) + sh


def _mod_kernel(c_ref, w_ref, b_ref, o_ref):
    o_ref[...] = jnp.dot(c_ref[...], w_ref[...], preferred_element_type=F32,
                         precision=lax.Precision.HIGHEST) + b_ref[...]


def _mod_call(c, w_ada, b_ada):
    b, d = c.shape
    n = w_ada.shape[1]
    tn = D_MODEL
    return pl.pallas_call(
        _mod_kernel,
        grid=(n // tn,),
        in_specs=[pl.BlockSpec((b, d), lambda j: (0, 0)),
                  pl.BlockSpec((d, tn), lambda j: (0, j)),
                  pl.BlockSpec((1, tn), lambda j: (0, j))],
        out_specs=pl.BlockSpec((b, tn), lambda j: (0, j)),
        out_shape=jax.ShapeDtypeStruct((b, n), F32),
        compiler_params=_cparams("parallel"),
        name="mod",
    )(c, w_ada, b_ada.reshape(1, n))


_P_QN, _P_QM, _P_KM, _P_VM = 0, 512, 1024, 1536
_P_KC, _P_VC, _P_KSL, _P_VSL, _P_KWN, _P_VWN, _P_GN = 2048, 2176, 2304, 2432, 2560, 2688, 2816
_P_WIDTH = 2944
ROPE_FREQS = HEAD_DIM // 2
ROPE_PACK = LANES // ROPE_FREQS


def _inproj_kernel(x_ref, sc_ref, sh_ref, g_ref, pos_ref, invf_ref, w_ref, gains_ref, bd_ref,
                   qn_ref, qm_ref, km_ref, vm_ref, kc_ref, vc_ref, ksl_ref, vsl_ref, kwn_ref,
                   vwn_ref, gn_ref, cos_ref, sin_ref):
    x = x_ref[0]
    tm = x.shape[0]
    h = _adaln_norm(x, g_ref[...], sc_ref[...], sh_ref[...]).astype(BF16)

    ang = pos_ref[0].astype(F32) * invf_ref[...]
    cos_p = jnp.cos(ang)
    sin_p = jnp.sin(ang)
    group = lax.broadcasted_iota(jnp.int32, ang.shape, 1) // ROPE_FREQS
    for k in range(ROPE_PACK):
        for packed, dst in ((cos_p, cos_ref), (sin_p, sin_ref)):
            own = jnp.where(group == k, packed, 0.0)
            spread = own
            for shift in range(1, ROPE_PACK):
                spread = spread + pltpu.roll(own, shift * ROPE_FREQS, 1)
            dst[pl.ds(k, tm // ROPE_PACK, stride=ROPE_PACK), :] = spread
    cos = cos_ref[...]
    sin = sin_ref[...]
    lane = lax.broadcasted_iota(jnp.int32, (tm, LANES), 1)
    first = (lane & (HEAD_DIM // 2)) == 0
    sin_signed = jnp.where(first, -sin, sin)
    bd = bd_ref[...]

    def head_norm(y, gain):
        ms = _dot((y * y).astype(BF16), bd)
        return y * lax.rsqrt(ms + NORM_EPS) * gain

    def rope(y):
        partner = jnp.where(first, pltpu.roll(y, LANES - HEAD_DIM // 2, 1),
                            pltpu.roll(y, HEAD_DIM // 2, 1))
        return y * cos + partner * sin_signed

    def proj(off, width):
        return _dot(h, w_ref[:, off:off + width])

    def wide(off, out_ref, gain_row, scale):
        acc = proj(off, 4 * LANES)
        gain = gains_ref[gain_row:gain_row + 1, :]
        for p in range(4):
            y = rope(head_norm(acc[:, p * LANES:(p + 1) * LANES], gain))
            if scale != 1.0:
                y = y * scale
            out_ref[0, :, p * LANES:(p + 1) * LANES] = y.astype(BF16)

    wide(_P_QN, qn_ref, 0, SCALE)
    wide(_P_QM, qm_ref, 1, SCALE)
    wide(_P_KM, km_ref, 2, 1.0)
    vm_ref[0] = proj(_P_VM, 4 * LANES).astype(BF16)

    small = proj(_P_KC, 7 * LANES)
    kc_ref[0] = rope(small[:, 0:LANES])
    vc_ref[0] = small[:, LANES:2 * LANES]
    ksl_ref[0] = rope(head_norm(small[:, 2 * LANES:3 * LANES], gains_ref[3:4, :])).astype(BF16)
    vsl_ref[0] = small[:, 3 * LANES:4 * LANES].astype(BF16)
    kwn_ref[0] = rope(head_norm(small[:, 4 * LANES:5 * LANES], gains_ref[4:5, :])).astype(BF16)
    vwn_ref[0] = small[:, 5 * LANES:6 * LANES].astype(BF16)
    gn_ref[0] = jax.nn.sigmoid(small[:, 6 * LANES:7 * LANES])


def _inproj_call(x, mod4, g_attn, pos_packed, inv_freq, w_p, gains, bd):
    b, s, d = x.shape
    tm = min(TM_INPROJ, s)
    tok = lambda width: pl.BlockSpec((1, tm, width), lambda bi, i: (bi, i, 0))
    modspec = lambda k: pl.BlockSpec((None, None, 1, d), lambda bi, i: (bi, k, 0, 0))
    shapes = [jax.ShapeDtypeStruct((b, s, 4 * LANES), BF16)] * 4 \
        + [jax.ShapeDtypeStruct((b, s, LANES), F32)] * 2 \
        + [jax.ShapeDtypeStruct((b, s, LANES), BF16)] * 4 \
        + [jax.ShapeDtypeStruct((b, s, LANES), F32)]
    return pl.pallas_call(
        _inproj_kernel,
        grid=(b, s // tm),
        in_specs=[tok(d), modspec(1), modspec(0), _const_spec((1, d)),
                  pl.BlockSpec((1, tm // ROPE_PACK, LANES), lambda bi, i: (bi, i, 0)),
                  _const_spec((1, LANES)), _const_spec((d, _P_WIDTH)),
                  _const_spec((8, LANES)), _const_spec((LANES, LANES))],
        out_specs=[tok(4 * LANES)] * 4 + [tok(LANES)] * 7,
        out_shape=shapes,
        scratch_shapes=[pltpu.VMEM((tm, LANES), F32)] * 2,
        compiler_params=_cparams("parallel", "parallel"),
        name="inproj",
    )(x, mod4, mod4, g_attn, pos_packed, inv_freq, w_p, gains, bd)


def _compress_kernel(xk_ref, xv_ref, posk_ref, posv_ref, w1k_ref, w1v_ref, w2k_ref, w2v_ref,
                     gain_ref, bd_ref, ko_ref, vo_ref):
    n_grp = ko_ref.shape[1]

    def mlp(x_ref, pos_ref, w1_ref, w2_ref):
        first = None
        second = None
        for l in range(NSA_CMP_STRIDE):
            xl = x_ref[0, pl.ds(l, n_grp, stride=NSA_CMP_STRIDE), :]
            a = _dot((xl + pos_ref[l:l + 1, :]).astype(BF16), w1_ref[l])
            lb = NSA_CMP_STRIDE + l
            b2 = _dot((xl + pos_ref[lb:lb + 1, :]).astype(BF16), w1_ref[lb])
            first = a if first is None else first + a
            second = b2 if second is None else second + b2
        hidden = first + pltpu.roll(second, n_grp - 1, 0)
        return _dot(jax.nn.gelu(hidden).astype(BF16), w2_ref[...])

    k = mlp(xk_ref, posk_ref, w1k_ref, w2k_ref)
    ms = _dot((k * k).astype(BF16), bd_ref[...])
    ko_ref[0] = (k * lax.rsqrt(ms + NORM_EPS) * gain_ref[...]).astype(BF16)
    vo_ref[0] = mlp(xv_ref, posv_ref, w1v_ref, w2v_ref).astype(BF16)


def _compress_call(xk, xv, posk, posv, w1k, w1v, w2k, w2v, gain, bd):
    b, s, w = xk.shape
    n = s // NSA_CMP_STRIDE
    xspec = pl.BlockSpec((1, s, w), lambda bi: (bi, 0, 0))
    ospec = pl.BlockSpec((1, n, LANES), lambda bi: (bi, 0, 0))
    return pl.pallas_call(
        _compress_kernel,
        grid=(b,),
        in_specs=[xspec, xspec, _const_spec(posk.shape), _const_spec(posv.shape),
                  _const_spec(w1k.shape), _const_spec(w1v.shape), _const_spec(w2k.shape),
                  _const_spec(w2v.shape), _const_spec((1, LANES)), _const_spec((LANES, LANES))],
        out_specs=[ospec, ospec],
        out_shape=[jax.ShapeDtypeStruct((b, n, LANES), BF16)] * 2,
        compiler_params=_cparams("parallel"),
        name="compress",
    )(xk, xv, posk, posv, w1k, w1v, w2k, w2v, gain, bd)


def _lane_half0(tq):
    return lax.broadcasted_iota(jnp.int32, (tq, LANES), 1) < HEAD_DIM


def _dim_half0(tq):
    return lax.broadcasted_iota(jnp.int32, (LANES, tq), 0) < HEAD_DIM


def _take_half(q2, half0, g):
    qf = q2.astype(F32)
    keep = half0 if g == 0 else jnp.logical_not(half0)
    return jnp.where(keep, qf, 0.0).astype(BF16)


def _transposed_bf16(rows):
    return rows.astype(F32).T.astype(BF16)


def _rank_counts(vals, jb, n):
    counts = jnp.zeros_like(vals)
    for j in range(n):
        row = vals[j:j + 1, :]
        beats = jnp.where(vals > row, 1.0,
                          jnp.where(vals == row, jnp.where(jb < j, 1.0, 0.0), 0.0))
        cnt = jnp.sum(beats, axis=0, keepdims=True)
        counts = jnp.where(jb == j, cnt, counts)
    return counts


def _values_t_with_ones(rows, dhalf0):
    vt = rows.astype(F32).T
    return (jnp.where(dhalf0, vt, 1.0).astype(BF16), jnp.where(dhalf0, 1.0, vt).astype(BF16))


def _softmax_max(parts):
    m = None
    for s, _, bias in parts:
        if bias is None:
            mx = jnp.max(s, axis=0, keepdims=True)
        else:
            n_blk = bias.shape[0]
            blk = s.shape[0] // n_blk
            mx = None
            for b in range(n_blk):
                mb = jnp.max(s[b * blk:(b + 1) * blk], axis=0, keepdims=True) + bias[b:b + 1]
                mx = mb if mx is None else jnp.maximum(mx, mb)
        m = mx if m is None else jnp.maximum(m, mx)
    return m


def _softmax_values(parts, m):
    acc = None
    for s, vt, bias in parts:
        if bias is None:
            p = jnp.exp2(s - m)
        else:
            n_blk = bias.shape[0]
            blk = s.shape[0] // n_blk
            p = jnp.concatenate([jnp.exp2(s[b * blk:(b + 1) * blk] - (m - bias[b:b + 1]))
                                 for b in range(n_blk)], axis=0)
        pv = _dot(vt, p.astype(BF16))
        acc = pv if acc is None else acc + pv
    return acc


def _solve_all(problems):
    all_parts = [problem() for problem in problems]
    maxima = [_softmax_max(parts) for parts in all_parts]
    return [_softmax_values(parts, m) for parts, m in zip(all_parts, maxima)]


def _normalized_pair(acc0, acc1, dhalf0):
    l0 = jnp.maximum(acc0[HEAD_DIM:HEAD_DIM + 1, :], 1e-30)
    l1 = jnp.maximum(acc1[0:1, :], 1e-30)
    return jnp.where(dhalf0, acc0 * (1.0 / l0), acc1 * (1.0 / l1))


NSA_TILES_PER_STEP = 4
MOBA_TILES_PER_STEP = 8


def _tile_groups(n_tiles, tiles_per_step):
    pairs = [[j + 1] if n_tiles - j == j + 1 else [j + 1, n_tiles - j]
             for j in range((n_tiles + 1) // 2)]
    per_group = max(tiles_per_step // 2, 1)
    return [sum(pairs[i:i + per_group], []) for i in range(0, len(pairs), per_group)]


def _grouped_tile_cases(step, groups, tiles_case):
    for j, tiles in enumerate(groups):
        pl.when(step == j)(lambda tiles=tiles: tiles_case(tiles))


def _nsa_cmp_kernel(q_ref, gate_ref, kc_ref, vc_ref, ovt_ref, egt_ref, ocmpt_ref, selt_ref):
    tq = q_ref.shape[1]
    n_c = kc_ref.shape[1]
    t0 = pl.program_id(1) * tq
    g_hi, g_lo = _split_bf16(gate_ref[0])

    c_row = lax.broadcasted_iota(jnp.int32, (n_c, tq), 0)
    t_col = t0 + lax.broadcasted_iota(jnp.int32, (n_c, tq), 1)
    visible = c_row * NSA_CMP_STRIDE + (NSA_CMP_BLOCK - 1) <= t_col
    half0 = _lane_half0(tq)
    dhalf0 = _dim_half0(tq)
    kc = kc_ref[0]
    vct = _transposed_bf16(vc_ref[0])

    n_pairs = NSA_HEADS // NSA_KV_HEADS
    scores = [[_dot_nt(kc, _take_half(q_ref[0, :, r * LANES:(r + 1) * LANES], half0, g))
               for g in range(NSA_KV_HEADS)] for r in range(n_pairs)]
    psum = [None, None]
    for r in range(n_pairs):
        o_pair = None
        for g in range(NSA_KV_HEADS):
            s = jnp.where(visible, scores[r][g], NEG_INF)
            m = jnp.max(s, axis=0, keepdims=True)
            p = jnp.where(visible, jnp.exp2(s - m), 0.0)
            p = p / jnp.maximum(jnp.sum(p, axis=0, keepdims=True), 1e-30)
            psum[g] = p if psum[g] is None else psum[g] + p
            o = _dot(vct, p.astype(BF16))
            o_pair = o if g == 0 else jnp.where(dhalf0, o_pair, o)
        egt = egt_ref[r * LANES:(r + 1) * LANES, :]
        gate_t = _dot_nt(egt, g_hi) + _dot_nt(egt, g_lo)
        ocmpt_ref[0, r] = gate_t * o_pair

    n_sb = ovt_ref.shape[0]
    n_top = min(NSA_SEL_TOPN, n_sb)
    few_candidates = (t0 + tq) // NSA_SEL_BLOCK <= n_top

    @pl.when(few_candidates)
    def _():
        selt_ref[...] = jnp.zeros_like(selt_ref)

    @pl.when(jnp.logical_not(few_candidates))
    def _():
        jb = lax.broadcasted_iota(jnp.int32, (n_sb, tq), 0)
        own = (t0 + lax.broadcasted_iota(jnp.int32, (n_sb, tq), 1)) // NSA_SEL_BLOCK
        forced = (jb == 0) | (jb == own) | (jb == own - 1)
        ovt = ovt_ref[...]
        for g in range(NSA_KV_HEADS):
            p_hi, p_lo = _split_bf16(psum[g])
            imp = _dot(ovt, p_hi) + _dot(ovt, p_lo)
            imp = jnp.where(jb <= own, imp + jnp.where(forced, NSA_FORCE_BONUS, 0.0), NEG_INF)
            counts = _rank_counts(imp, jb, n_sb)
            selt_ref[0, g] = jnp.where(counts < float(n_top), 0.0, MASK_BIAS)


def _nsa_cmp_call(qn, gn, kcmp, vcmp, ovt, eg_cmp_t):
    b, s, w = qn.shape
    tq = min(TQ_CMP, s)
    n_pairs = w // LANES
    n_sb = ovt.shape[0]
    tok = lambda width: pl.BlockSpec((1, tq, width), lambda bi, i: (bi, i, 0))
    cspec = pl.BlockSpec((1, kcmp.shape[1], LANES), lambda bi, i: (bi, 0, 0))
    return pl.pallas_call(
        _nsa_cmp_kernel,
        grid=(b, s // tq),
        in_specs=[tok(w), tok(LANES), cspec, cspec, _const_spec(ovt.shape),
                  _const_spec(eg_cmp_t.shape)],
        out_specs=[pl.BlockSpec((1, n_pairs, LANES, tq), lambda bi, i: (bi, 0, 0, i)),
                   pl.BlockSpec((1, NSA_KV_HEADS, n_sb, tq), lambda bi, i: (bi, 0, 0, i))],
        out_shape=[jax.ShapeDtypeStruct((b, n_pairs, LANES, s), F32),
                   jax.ShapeDtypeStruct((b, NSA_KV_HEADS, n_sb, s), F32)],
        compiler_params=_cparams("parallel", "parallel"),
        name="nsa_cmp",
    )(qn, gn, kcmp, vcmp, ovt, eg_cmp_t)


def _nsa_attn_kernel(q_ref, selt_ref, gate_ref, ocmpt_ref, ksl_ref, vsl_ref, kwn_ref, vwn_ref,
                     egst_ref, egwt_ref, o_ref, vslt_ref, vwnt_ref):
    s_len = q_ref.shape[1]
    tq = min(TQ, s_len)
    n_tiles = s_len // tq
    step = pl.program_id(2)
    half0 = _lane_half0(tq)
    dhalf0 = _dim_half0(tq)
    k_row = lax.broadcasted_iota(jnp.int32, (tq, tq), 0)
    q_col = lax.broadcasted_iota(jnp.int32, (tq, tq), 1)
    causal = k_row <= q_col
    blocks_per_chunk = tq // NSA_SEL_BLOCK

    @pl.when(jnp.logical_and(pl.program_id(1) == 0, step == 0))
    def _():
        for c in range(n_tiles):
            cols = slice(c * tq, (c + 1) * tq)
            vslt_ref[0, :, cols], vslt_ref[1, :, cols] = _values_t_with_ones(vsl_ref[0, cols, :],
                                                                             dhalf0)
            vwnt_ref[0, :, cols], vwnt_ref[1, :, cols] = _values_t_with_ones(vwn_ref[0, cols, :],
                                                                             dhalf0)

    def sel_problem(n, qh, g):
        lo = (n - 1) * tq
        all_selected = n * blocks_per_chunk <= NSA_SEL_TOPN
        parts = []
        for c in range(n):
            k0 = c * tq
            sc = _dot_nt(ksl_ref[0, k0:k0 + tq, :], qh)
            if c == n - 1:
                sc = jnp.where(causal, sc, NEG_INF)
            bias = None
            if not all_selected:
                j0 = c * blocks_per_chunk
                bias = selt_ref[0, g, j0:j0 + blocks_per_chunk, lo:lo + tq]
            parts.append((sc, vslt_ref[g, :, k0:k0 + tq], bias))
        return parts

    def win_problem(n, qh, g):
        lo = (n - 1) * tq
        parts = [(jnp.where(causal, _dot_nt(kwn_ref[0, lo:lo + tq, :], qh), NEG_INF),
                  vwnt_ref[g, :, lo:lo + tq], None)]
        for back in range(1, min(n - 1, (NSA_WINDOW + tq - 2) // tq) + 1):
            k0 = lo - back * tq
            sw = _dot_nt(kwn_ref[0, k0:k0 + tq, :], qh)
            if (back + 1) * tq - 1 >= NSA_WINDOW:
                sw = jnp.where(q_col - k_row + back * tq < NSA_WINDOW, sw, NEG_INF)
            parts.append((sw, vwnt_ref[g, :, k0:k0 + tq], None))
        return parts

    def tiles_case(tiles):
        problems = []
        for n in tiles:
            q2 = q_ref[0, (n - 1) * tq:n * tq, :]
            for g in range(NSA_KV_HEADS):
                qh = _take_half(q2, half0, g)
                problems.append(lambda n=n, qh=qh, g=g: sel_problem(n, qh, g))
                problems.append(lambda n=n, qh=qh, g=g: win_problem(n, qh, g))
        accs = _solve_all(problems)
        for i, n in enumerate(tiles):
            lo = (n - 1) * tq
            sel0, win0, sel1, win1 = accs[4 * i:4 * i + 4]
            g_hi, g_lo = _split_bf16(gate_ref[0, lo:lo + tq, :])
            gs_t = _dot_nt(egst_ref[...], g_hi) + _dot_nt(egst_ref[...], g_lo)
            gw_t = _dot_nt(egwt_ref[...], g_hi) + _dot_nt(egwt_ref[...], g_lo)
            out_t = (ocmpt_ref[0, 0, :, lo:lo + tq] + gs_t * _normalized_pair(sel0, sel1, dhalf0)
                     + gw_t * _normalized_pair(win0, win1, dhalf0))
            o_ref[0, lo:lo + tq, :] = out_t.T.astype(BF16)

    _grouped_tile_cases(step, _tile_groups(n_tiles, NSA_TILES_PER_STEP), tiles_case)


def _nsa_attn_call(qn, selt, gn, ocmpt, ksl, vsl, kwn, vwn, eg_sel_t, eg_win_t):
    b, s, _ = qn.shape
    n_tiles = s // min(TQ, s)
    n_pairs = qn.shape[2] // LANES
    pair = pl.BlockSpec((1, s, LANES), lambda bi, r, j: (bi, 0, r))
    seq = pl.BlockSpec((1, s, LANES), lambda bi, r, j: (bi, 0, 0))
    egspec = pl.BlockSpec((LANES, LANES), lambda bi, r, j: (r, 0))
    return pl.pallas_call(
        _nsa_attn_kernel,
        grid=(b, n_pairs, len(_tile_groups(n_tiles, NSA_TILES_PER_STEP))),
        in_specs=[pair,
                  pl.BlockSpec((1,) + selt.shape[1:], lambda bi, r, j: (bi, 0, 0, 0)),
                  seq,
                  pl.BlockSpec((1, 1, LANES, s), lambda bi, r, j: (bi, r, 0, 0)),
                  seq, seq, seq, seq, egspec, egspec],
        out_specs=pair,
        out_shape=jax.ShapeDtypeStruct(qn.shape, BF16),
        scratch_shapes=[pltpu.VMEM((NSA_KV_HEADS, LANES, s), BF16)] * 2,
        compiler_params=_cparams("parallel", "arbitrary", "arbitrary"),
        name="nsa_attn",
    )(qn, selt, gn, ocmpt, ksl, vsl, kwn, vwn, eg_sel_t, eg_win_t)


def _moba_kernel(q_ref, k_ref, v_ref, o_ref, vt_ref, kmean_ref):
    s_len = q_ref.shape[1]
    tq = min(MOBA_BLOCK, s_len)
    nb = s_len // tq
    n_top = min(MOBA_TOPK, nb - 1)
    step = pl.program_id(2)
    half0 = _lane_half0(tq)
    dhalf0 = _dim_half0(tq)
    k_row = lax.broadcasted_iota(jnp.int32, (tq, tq), 0)
    q_col = lax.broadcasted_iota(jnp.int32, (tq, tq), 1)
    causal = k_row <= q_col
    jb = lax.broadcasted_iota(jnp.int32, (nb, tq), 0)

    @pl.when(step == 0)
    def _():
        for j in range(nb):
            rows = slice(j * tq, (j + 1) * tq)
            kmean_ref[j:j + 1, :] = jnp.mean(k_ref[0, rows, :].astype(F32), axis=0, keepdims=True)
            vt_ref[0, :, rows], vt_ref[1, :, rows] = _values_t_with_ones(v_ref[0, rows, :], dhalf0)

    def problem(n, qh, h):
        lo = (n - 1) * tq
        n_past = n - 1
        bias = None
        if n_past > n_top:
            km_hi, km_lo = _split_bf16(kmean_ref[...])
            gate = _dot_nt(km_hi, qh) + _dot_nt(km_lo, qh)
            gate = jnp.where(jb < n_past, gate, NEG_INF)
            counts = _rank_counts(gate, jb, n_past)
            bias = jnp.where(counts < float(n_top), 0.0, MASK_BIAS)
        parts = []
        for j in range(n_past):
            sj = _dot_nt(k_ref[0, j * tq:(j + 1) * tq, :], qh)
            parts.append((sj, vt_ref[h, :, j * tq:(j + 1) * tq],
                          None if bias is None else bias[j:j + 1, :]))
        s_own = _dot_nt(k_ref[0, lo:lo + tq, :], qh)
        parts.append((jnp.where(causal, s_own, NEG_INF), vt_ref[h, :, lo:lo + tq], None))
        return parts

    def tiles_case(tiles):
        problems = []
        for n in tiles:
            q2 = q_ref[0, (n - 1) * tq:n * tq, :]
            for h in range(2):
                qh = _take_half(q2, half0, h)
                problems.append(lambda n=n, qh=qh, h=h: problem(n, qh, h))
        accs = _solve_all(problems)
        for i, n in enumerate(tiles):
            out_t = _normalized_pair(accs[2 * i], accs[2 * i + 1], dhalf0)
            o_ref[0, (n - 1) * tq:n * tq, :] = out_t.T.astype(BF16)

    _grouped_tile_cases(step, _tile_groups(nb, MOBA_TILES_PER_STEP), tiles_case)


def _moba_call(qm, km, vm):
    b, s, w = qm.shape
    nb = s // min(MOBA_BLOCK, s)
    n_pairs = w // LANES
    seq = pl.BlockSpec((1, s, LANES), lambda bi, p, j: (bi, 0, p))
    return pl.pallas_call(
        _moba_kernel,
        grid=(b, n_pairs, len(_tile_groups(nb, MOBA_TILES_PER_STEP))),
        in_specs=[seq, seq, seq],
        out_specs=seq,
        out_shape=jax.ShapeDtypeStruct(qm.shape, BF16),
        scratch_shapes=[pltpu.VMEM((2, LANES, s), BF16), pltpu.VMEM((nb, LANES), F32)],
        compiler_params=_cparams("parallel", "parallel", "arbitrary"),
        name="moba",
    )(qm, km, vm)


def _merge_kernel(x_ref, sc_ref, sh_ref, gt_ref, g_ref, oa_ref, ob_ref, wga_ref, wgb_ref,
                  wbn_ref, wbm_ref, wo_ref, o_ref):
    x = x_ref[0]
    h = _adaln_norm(x, g_ref[...], sc_ref[...], sh_ref[...]).astype(BF16)
    ga = jax.nn.sigmoid(_dot(h, wga_ref[...]))
    gb = jax.nn.sigmoid(_dot(h, wgb_ref[...]))
    mixed = ga * _dot(oa_ref[0], wbn_ref[...]) + gb * _dot(ob_ref[0], wbm_ref[...])
    o_ref[0] = x + gt_ref[...] * _dot(mixed.astype(BF16), wo_ref[...])


def _merge_call(x, mod4, g_attn, oa, ob, wga, wgb, wbn, wbm, wo):
    b, s, d = x.shape
    tm = min(TM_INPROJ, s)
    tok = lambda width: pl.BlockSpec((1, tm, width), lambda bi, i: (bi, i, 0))
    modspec = lambda k: pl.BlockSpec((None, None, 1, d), lambda bi, i: (bi, k, 0, 0))
    return pl.pallas_call(
        _merge_kernel,
        grid=(b, s // tm),
        in_specs=[tok(d), modspec(1), modspec(0), modspec(2), _const_spec((1, d)),
                  tok(oa.shape[2]), tok(ob.shape[2]), _const_spec(wga.shape),
                  _const_spec(wgb.shape), _const_spec(wbn.shape), _const_spec(wbm.shape),
                  _const_spec(wo.shape)],
        out_specs=tok(d),
        out_shape=jax.ShapeDtypeStruct(x.shape, F32),
        compiler_params=_cparams("parallel", "parallel"),
        name="merge",
    )(x, mod4, mod4, mod4, g_attn, oa, ob, wga, wgb, wbn, wbm, wo)


HALO = 8


def _ffn_kernel(x_ref, halo_ref, sc_ref, sh_ref, gt_ref, g_ref, wup_ref, cw_ref, cb_ref,
                wd_ref, o_ref):
    x = x_ref[0]
    tm = x.shape[0]
    g = g_ref[...]
    h = _adaln_norm(x, g, sc_ref[...], sh_ref[...]).astype(BF16)
    h_halo = _adaln_norm(halo_ref[0], g, sc_ref[...], sh_ref[...]).astype(BF16)
    h_ext = jnp.concatenate([h_halo, h], axis=0)
    ext_row = lax.broadcasted_iota(jnp.int32, (HALO + tm, FF_CHUNK), 0)
    live = ext_row >= jnp.where(pl.program_id(1) > 0, 0, HALO)

    d_ff = wd_ref.shape[0]
    n_chunks = d_ff // FF_CHUNK
    cols = lambda c: slice(c * FF_CHUNK, (c + 1) * FF_CHUNK)

    def up(c):
        a = _dot(h_ext, wup_ref[:, cols(c)])
        v = _dot(h, wup_ref[:, d_ff + c * FF_CHUNK:d_ff + (c + 1) * FF_CHUNK])
        return jnp.where(live, a, 0.0), v

    acc = None
    ups = [up(c) for c in range(n_chunks)]
    for c in range(n_chunks):
        a, v = ups[c]
        cw = cw_ref[:, cols(c)]
        y = cb_ref[:, cols(c)] + cw[CONV_WIDTH - 1:CONV_WIDTH, :] * a[HALO:, :]
        for back in range(1, CONV_WIDTH):
            k = CONV_WIDTH - 1 - back
            y = y + cw[k:k + 1, :] * pltpu.roll(a, back, 0)[HALO:, :]
        gated = jax.nn.gelu(y) * v
        part = _dot(gated.astype(BF16), wd_ref[cols(c), :])
        acc = part if acc is None else acc + part
    o_ref[0] = x + gt_ref[...] * acc


def _ffn_call(x, mod4, g_ffn, wup, cw, cb, wd):
    b, s, d = x.shape
    tm = min(TM_PROJ, s)
    tok = pl.BlockSpec((1, tm, d), lambda bi, i: (bi, i, 0))
    halo = pl.BlockSpec((1, HALO, d), lambda bi, i: (bi, jnp.maximum(i * (tm // HALO) - 1, 0), 0))
    modspec = lambda k: pl.BlockSpec((None, None, 1, d), lambda bi, i: (bi, k, 0, 0))
    return pl.pallas_call(
        _ffn_kernel,
        grid=(b, s // tm),
        in_specs=[tok, halo, modspec(4), modspec(3), modspec(5), _const_spec((1, d)),
                  _const_spec(wup.shape), _const_spec(cw.shape), _const_spec(cb.shape),
                  _const_spec(wd.shape)],
        out_specs=tok,
        out_shape=jax.ShapeDtypeStruct(x.shape, F32),
        compiler_params=_cparams("parallel", "parallel"),
        name="ffn",
    )(x, x, mod4, mod4, mod4, g_ffn, wup, cw, cb, wd)


def _block_diag_mean():
    lane = np.arange(LANES)
    return (lane[:, None] // HEAD_DIM == lane[None, :] // HEAD_DIM).astype(np.float32) / HEAD_DIM


def _overlap_t(n_cmp, n_cmp_pad, n_sb):
    c_start = np.arange(n_cmp)[:, None] * NSA_CMP_STRIDE
    js = np.arange(n_sb)[None, :]
    ov = ((c_start < (js + 1) * NSA_SEL_BLOCK) & (c_start + NSA_CMP_BLOCK > js * NSA_SEL_BLOCK))
    out = np.zeros((n_sb, n_cmp_pad), np.float32)
    out[:, :n_cmp] = ov.T
    return out


def _gate_expand_t(branch):
    out = np.zeros((NSA_WIDTH, LANES), np.float32)
    for slot, head in enumerate(NSA_HEAD_ORDER):
        out[slot * HEAD_DIM:(slot + 1) * HEAD_DIM, 3 * head + branch] = 1.0
    return out


def _block_diag2(w):
    z = jnp.zeros_like(w)
    return jnp.concatenate([jnp.concatenate([w, z], axis=1), jnp.concatenate([z, w], axis=1)], axis=0)


def _layer(x, c, positions, w_ada, b_ada, g_attn_norm, w_in, g_q_nsa, g_k_cmp, g_k_slc, g_k_win,
           cmp_k_pos, cmp_k_w1, cmp_k_w2, cmp_v_pos, cmp_v_w1, cmp_v_w2, g_q_moba, g_k_moba,
           w_branch_nsa, w_branch_moba, w_out, g_ffn_norm, w_ffn_up, conv_w, conv_b, w_ffn_down):
    b, s, d = x.shape
    order = np.asarray(NSA_HEAD_ORDER)
    off = IN_OFFSETS

    mod = _mod_call(c, w_ada, b_ada)
    mod4 = mod.reshape(b, 6, 1, d)

    col = lambda k: w_in[:, off[k]:off[k + 1]]
    w_qn = col(0).reshape(d, NSA_HEADS, HEAD_DIM)[:, order].reshape(d, NSA_WIDTH)
    w_gn = jnp.pad(col(7), ((0, 0), (0, LANES - 3 * NSA_HEADS)))
    w_p = jnp.concatenate([w_qn, col(8), col(9), col(10), col(1), col(2), col(3), col(4), col(5),
                           col(6), w_gn], axis=1).astype(BF16)
    tile2 = lambda gv: jnp.tile(gv, LANES // HEAD_DIM)
    gains = jnp.stack([tile2(g_q_nsa), tile2(g_q_moba), tile2(g_k_moba), tile2(g_k_slc),
                       tile2(g_k_win)] + [jnp.ones((LANES,), F32)] * 3)
    bd = jnp.asarray(_block_diag_mean(), BF16)
    half = HEAD_DIM // 2
    inv_freq = ROPE_THETA ** (-jnp.arange(half, dtype=F32) / half)
    inv_freq = jnp.tile(inv_freq, LANES // half).reshape(1, LANES)
    pos_packed = jnp.repeat(positions.reshape(b, s // ROPE_PACK, ROPE_PACK), ROPE_FREQS, axis=2)

    (qn, qm, km, vm, kc, vc, ksl, vsl, kwn, vwn, gn) = _inproj_call(
        x, mod4, g_attn_norm.reshape(1, d), pos_packed, inv_freq, w_p, gains, bd)

    n_grp = s // NSA_CMP_STRIDE
    n_cmp = (s - NSA_CMP_BLOCK) // NSA_CMP_STRIDE + 1
    w1_pack = lambda w1: jax.vmap(_block_diag2)(
        w1.reshape(NSA_CMP_BLOCK, HEAD_DIM, HEAD_DIM)).astype(BF16)
    pos_pack = lambda p: jnp.tile(p, (1, NSA_KV_HEADS))
    kcmp, vcmp = _compress_call(
        kc, vc, pos_pack(cmp_k_pos), pos_pack(cmp_v_pos),
        w1_pack(cmp_k_w1), w1_pack(cmp_v_w1), _block_diag2(cmp_k_w2).astype(BF16),
        _block_diag2(cmp_v_w2).astype(BF16), tile2(g_k_cmp).reshape(1, LANES), bd)

    n_sb = s // NSA_SEL_BLOCK
    ovt = jnp.asarray(_overlap_t(n_cmp, n_grp, n_sb), BF16)
    ocmpt, selt = _nsa_cmp_call(qn, gn, kcmp, vcmp, ovt, jnp.asarray(_gate_expand_t(0), BF16))
    oa = _nsa_attn_call(qn, selt, gn, ocmpt, ksl, vsl, kwn, vwn,
                        jnp.asarray(_gate_expand_t(1), BF16), jnp.asarray(_gate_expand_t(2), BF16))

    ob = _moba_call(qm, km, vm)

    w_bn = w_branch_nsa.reshape(NSA_HEADS, HEAD_DIM, d)[order].reshape(NSA_WIDTH, d)
    x1 = _merge_call(x, mod4, g_attn_norm.reshape(1, d), oa, ob, col(11).astype(BF16),
                     col(12).astype(BF16), w_bn.astype(BF16), w_branch_moba.astype(BF16),
                     w_out.astype(BF16))

    return _ffn_call(x1, mod4, g_ffn_norm.reshape(1, d), w_ffn_up.astype(BF16), conv_w,
                     conv_b.reshape(1, -1), w_ffn_down.astype(BF16))


def kernel(x, c, positions, w_ada, b_ada, g_attn_norm, w_in, g_q_nsa, g_k_cmp, g_k_slc, g_k_win,
           cmp_k_pos, cmp_k_w1, cmp_k_w2, cmp_v_pos, cmp_v_w1, cmp_v_w2, g_q_moba, g_k_moba,
           w_branch_nsa, w_branch_moba, w_out, g_ffn_norm, w_ffn_up, conv_w, conv_b, w_ffn_down):
    for l in range(w_ada.shape[0]):
        x = _layer(x, c, positions, w_ada[l], b_ada[l], g_attn_norm[l], w_in[l], g_q_nsa[l],
                   g_k_cmp[l], g_k_slc[l], g_k_win[l], cmp_k_pos[l], cmp_k_w1[l], cmp_k_w2[l],
                   cmp_v_pos[l], cmp_v_w1[l], cmp_v_w2[l], g_q_moba[l], g_k_moba[l],
                   w_branch_nsa[l], w_branch_moba[l], w_out[l], g_ffn_norm[l], w_ffn_up[l],
                   conv_w[l], conv_b[l], w_ffn_down[l])
    return x
```
